```python
import math
import jax, jax.numpy as jnp
from jax import lax
import numpy as np

D_MODEL = 1024
BATCH = 16
SEQ = 2048
DEPTH = 1

ATTN_HEADS = 8
HEAD_DIM = 64
ATTN_WIDTH = ATTN_HEADS * HEAD_DIM
GMLP_GROUPS = 4
GMLP_GROUP_DIM = 128
GMLP_WIDTH = GMLP_GROUPS * GMLP_GROUP_DIM
MIX_WIDTH = ATTN_WIDTH + GMLP_WIDTH
IN_WIDTH = 3 * ATTN_WIDTH + 2 * GMLP_WIDTH
CHUNK = 128
WINDOW_DILATIONS = ((128, 1), (512, 4), (2048, 16))
BLOCK = 128
D_FF = 4 * D_MODEL
EPS = 1e-6

kernel_name = "hybrid_dilated_attn_gmlp_block"


def rms_norm(x, g):
    xf = x.astype(jnp.float32)
    y = xf * lax.rsqrt(jnp.mean(xf * xf, axis=-1, keepdims=True) + EPS)
    return (y * g.astype(jnp.float32)).astype(x.dtype)


def layer_norm(x, g, b):
    xf = x.astype(jnp.float32)
    mu = jnp.mean(xf, axis=-1, keepdims=True)
    var = jnp.mean(jnp.square(xf - mu), axis=-1, keepdims=True)
    y = (xf - mu) * lax.rsqrt(var + EPS)
    return (y * g.astype(jnp.float32) + b.astype(jnp.float32)).astype(x.dtype)


def causal_window_attention(q, k, v, span):
    N, L, H, D = q.shape
    Lp = -(-L // BLOCK) * BLOCK
    pad = Lp - L
    if pad:
        pw = ((0, 0), (0, pad), (0, 0), (0, 0))
        q, k, v = jnp.pad(q, pw), jnp.pad(k, pw), jnp.pad(v, pw)
    nb = Lp // BLOCK
    qb = q.reshape(N, nb, BLOCK, H, D)
    kb = k.reshape(N, nb, BLOCK, H, D)
    vb = v.reshape(N, nb, BLOCK, H, D)
    k_prev = jnp.concatenate([jnp.zeros_like(kb[:, :1]), kb[:, :-1]], axis=1)
    v_prev = jnp.concatenate([jnp.zeros_like(vb[:, :1]), vb[:, :-1]], axis=1)
    kk = jnp.concatenate([k_prev, kb], axis=2)
    vv = jnp.concatenate([v_prev, vb], axis=2)
    scale = 1.0 / math.sqrt(D)
    s = jnp.einsum('nbqhd,nbkhd->nbhqk', qb, kk).astype(jnp.float32) * scale
    qpos = jnp.arange(BLOCK)[:, None] + BLOCK
    kpos = jnp.arange(2 * BLOCK)[None, :]
    dist = qpos - kpos
    band = (dist >= 0) & (dist <= span)
    first = (jnp.arange(nb)[:, None, None] == 0) & (kpos[None] < BLOCK)
    mask = band[None] & ~first
    s = jnp.where(mask[None, :, None], s, -jnp.inf)
    m = jnp.max(s, axis=-1, keepdims=True)
    p = jnp.exp(s - m)
    den = jnp.sum(p, axis=-1)
    o = jnp.einsum('nbhqk,nbkhd->nbqhd', p.astype(vv.dtype), vv).astype(jnp.float32)
    o = o / jnp.transpose(den, (0, 1, 3, 2))[..., None]
    lse = m[..., 0] + jnp.log(den)
    o = o.reshape(N, Lp, H, D)[:, :L]
    lse = jnp.transpose(lse, (0, 1, 3, 2)).reshape(N, Lp, H)[:, :L]
    return o, lse


def dilated_branch(q, k, v, window, dilation):
    B, S, H, D = q.shape
    L = S // dilation

    def to_residue(t):
        return t.reshape(B, L, dilation, H, D).transpose(0, 2, 1, 3, 4).reshape(B * dilation, L, H, D)

    o, lse = causal_window_attention(to_residue(q), to_residue(k), to_residue(v), window // dilation)
    o = o.reshape(B, dilation, L, H, D).transpose(0, 2, 1, 3, 4).reshape(B, S, H, D)
    lse = lse.reshape(B, dilation, L, H).transpose(0, 2, 1, 3).reshape(B, S, H)
    return o, lse


def dilated_attention(q, k, v):
    outs, lses = [], []
    for window, dilation in WINDOW_DILATIONS:
        o, lse = dilated_branch(q, k, v, window, dilation)
        outs.append(o)
        lses.append(lse)
    o = jnp.stack(outs, axis=0)
    w = jax.nn.softmax(jnp.stack(lses, axis=0), axis=0)
    return jnp.sum(w[..., None] * o, axis=0)


def spatial_gating(u, g, ln_g, ln_b, w_s, b_s):
    B, S, _ = u.shape
    nc = S // CHUNK
    g = g.reshape(B, S, GMLP_GROUPS, GMLP_GROUP_DIM)
    g = layer_norm(g, ln_g, ln_b)
    g = g.reshape(B, nc, CHUNK, GMLP_GROUPS, GMLP_GROUP_DIM)
    causal = jnp.tril(jnp.ones((CHUNK, CHUNK), dtype=bool))
    w_m = jnp.where(causal[None], w_s, 0.0).astype(g.dtype)
    z = jnp.einsum('gts,bnsgc->bntgc', w_m, g) + b_s.T[None, None, :, :, None]
    z = z.reshape(B, S, GMLP_WIDTH)
    return u * z


def setup_inputs(seed: int = 0) -> dict:
    key = jax.random.key(seed)
    ks = jax.random.split(key, 16)
    f32 = jnp.float32

    def nrm(k, shape, scale):
        return jax.random.normal(k, shape, f32) * scale

    L = DEPTH
    return {
        "x": jax.random.normal(ks[0], (BATCH, SEQ, D_MODEL), f32),
        "norm1_g": 1.0 + nrm(ks[1], (L, D_MODEL), 0.02),
        "w_in": nrm(ks[2], (L, D_MODEL, IN_WIDTH), D_MODEL ** -0.5),
        "q_norm_g": 1.0 + nrm(ks[3], (L, HEAD_DIM), 0.02),
        "k_norm_g": 1.0 + nrm(ks[4], (L, HEAD_DIM), 0.02),
        "ln_v_g": 1.0 + nrm(ks[5], (L, GMLP_GROUPS, GMLP_GROUP_DIM), 0.02),
        "ln_v_b": nrm(ks[6], (L, GMLP_GROUPS, GMLP_GROUP_DIM), 0.02),
        "w_spatial": nrm(ks[7], (L, GMLP_GROUPS, CHUNK, CHUNK), CHUNK ** -0.5),
        "b_spatial": 1.0 + nrm(ks[8], (L, GMLP_GROUPS, CHUNK), 0.1),
        "attn_out_g": 1.0 + nrm(ks[9], (L, ATTN_WIDTH), 0.02),
        "gmlp_out_g": 1.0 + nrm(ks[10], (L, GMLP_WIDTH), 0.02),
        "w_out": nrm(ks[11], (L, MIX_WIDTH, D_MODEL), MIX_WIDTH ** -0.5),
        "norm2_g": 1.0 + nrm(ks[12], (L, D_MODEL), 0.02),
        "w_ff1": nrm(ks[13], (L, D_MODEL, D_FF), D_MODEL ** -0.5),
        "w_ff2": nrm(ks[14], (L, D_FF, D_MODEL), D_FF ** -0.5),
    }


def reference(x, norm1_g, w_in, q_norm_g, k_norm_g, ln_v_g, ln_v_b, w_spatial,
              b_spatial, attn_out_g, gmlp_out_g, w_out, norm2_g, w_ff1, w_ff2):
    B, S, _ = x.shape
    for l in range(DEPTH):
        h = rms_norm(x, norm1_g[l])
        proj = jnp.einsum('bsd,de->bse', h, w_in[l])
        q, k, v, u, g = jnp.split(
            proj, np.cumsum([ATTN_WIDTH] * 3 + [GMLP_WIDTH]).tolist(), axis=-1)
        q = rms_norm(q.reshape(B, S, ATTN_HEADS, HEAD_DIM), q_norm_g[l])
        k = rms_norm(k.reshape(B, S, ATTN_HEADS, HEAD_DIM), k_norm_g[l])
        v = v.reshape(B, S, ATTN_HEADS, HEAD_DIM)
        a = dilated_attention(q, k, v).astype(x.dtype).reshape(B, S, ATTN_WIDTH)
        m = spatial_gating(jax.nn.gelu(u), jax.nn.gelu(g), ln_v_g[l], ln_v_b[l],
                           w_spatial[l], b_spatial[l])
        mix = jnp.concatenate([rms_norm(a, attn_out_g[l]), rms_norm(m, gmlp_out_g[l])], axis=-1)
        x = x + jnp.einsum('bse,ed->bsd', mix, w_out[l])
        h = rms_norm(x, norm2_g[l])
        f = jnp.square(jax.nn.relu(jnp.einsum('bsd,df->bsf', h, w_ff1[l])))
        x = x + jnp.einsum('bsf,fd->bsd', f, w_ff2[l])
    return x
```

```python
import functools
import math

import jax
import jax.numpy as jnp
from jax import lax
from jax.experimental import pallas as pl
from jax.experimental.pallas import tpu as pltpu

ATTN_HEADS = 8
HEAD_DIM = 64
ATTN_WIDTH = ATTN_HEADS * HEAD_DIM
GMLP_GROUPS = 4
GMLP_GROUP_DIM = 128
GMLP_WIDTH = GMLP_GROUPS * GMLP_GROUP_DIM
CHUNK = 128
BLOCK = 128
WINDOW_DILATIONS = ((128, 1), (512, 4), (2048, 16))
EPS = 1e-6

LANES = 128
HEADS_PER_LANE_GROUP = LANES // HEAD_DIM
LANE_GROUPS = ATTN_WIDTH // LANES
PROJ_ROWS = 512
MLP_ROWS = 512
FF_CHUNK = 1024
VMEM_LIMIT_BYTES = 56 * 1024 * 1024

F32 = jnp.float32
BF16 = jnp.bfloat16


def _const_spec(shape):
    return pl.BlockSpec(shape, lambda *_: (0,) * len(shape))


def _proj_kernel(x_ref, g1_ref, w_ref, bd_ref, qg_ref, kg_ref, lng_ref, lnb_ref, ws_ref, bs_ref,
                 mg_ref, q_out, k_out, v_out, m_out, m_scr):
    xf = x_ref[...]
    ms = jnp.mean(xf * xf, axis=-1, keepdims=True)
    h = (xf * lax.rsqrt(ms + EPS) * g1_ref[...]).astype(BF16)

    def proj(c0, n):
        return jnp.dot(h, w_ref[:, c0:c0 + n], preferred_element_type=F32)

    def head_norm(t, g_ref, out_ref):
        msq = jnp.dot((t * t).astype(BF16), bd_ref[...], preferred_element_type=F32)
        out_ref[...] = (t * lax.rsqrt(msq + EPS) * g_ref[...]).astype(BF16)

    head_norm(proj(0, ATTN_WIDTH), qg_ref, q_out)
    head_norm(proj(ATTN_WIDTH, ATTN_WIDTH), kg_ref, k_out)
    v_out[...] = proj(2 * ATTN_WIDTH, ATTN_WIDTH).astype(BF16)

    u = jax.nn.gelu(proj(3 * ATTN_WIDTH, GMLP_WIDTH))
    g = jax.nn.gelu(proj(3 * ATTN_WIDTH + GMLP_WIDTH, GMLP_WIDTH))
    for gi in range(GMLP_GROUPS):
        cols = slice(gi * GMLP_GROUP_DIM, (gi + 1) * GMLP_GROUP_DIM)
        gg = g[:, cols]
        mu = jnp.mean(gg, axis=-1, keepdims=True)
        xc = gg - mu
        var = jnp.mean(xc * xc, axis=-1, keepdims=True)
        y = (xc * lax.rsqrt(var + EPS) * lng_ref[:, cols] + lnb_ref[:, cols]).astype(BF16)
        for c in range(PROJ_ROWS // CHUNK):
            rows = slice(c * CHUNK, (c + 1) * CHUNK)
            z = jnp.dot(ws_ref[gi], y[rows, :], preferred_element_type=F32) + bs_ref[:, cols]
            m_scr[rows, cols] = u[rows, cols] * z
    mm = m_scr[...]
    mms = jnp.mean(mm * mm, axis=-1, keepdims=True)
    m_out[...] = (mm * lax.rsqrt(mms + EPS) * mg_ref[...]).astype(BF16)


def _proj_call(x2, g1, w_in, bd, qg, kg, lng, lnb, ws, bs, mg):
    n_tok, d_model = x2.shape
    in_width = w_in.shape[1]
    out_sd = jax.ShapeDtypeStruct((n_tok, ATTN_WIDTH), BF16)
    row_spec = pl.BlockSpec((PROJ_ROWS, ATTN_WIDTH), lambda i: (i, 0))
    return pl.pallas_call(
        _proj_kernel,
        grid=(n_tok // PROJ_ROWS,),
        in_specs=[
            pl.BlockSpec((PROJ_ROWS, d_model), lambda i: (i, 0)),
            _const_spec((1, d_model)),
            _const_spec((d_model, in_width)),
            _const_spec((ATTN_WIDTH, ATTN_WIDTH)),
            _const_spec((1, ATTN_WIDTH)),
            _const_spec((1, ATTN_WIDTH)),
            _const_spec((1, GMLP_WIDTH)),
            _const_spec((1, GMLP_WIDTH)),
            _const_spec((GMLP_GROUPS, CHUNK, CHUNK)),
            _const_spec((CHUNK, GMLP_WIDTH)),
            _const_spec((1, GMLP_WIDTH)),
        ],
        out_specs=[row_spec, row_spec, row_spec, row_spec],
        out_shape=[out_sd, out_sd, out_sd, out_sd],
        scratch_shapes=[pltpu.VMEM((PROJ_ROWS, GMLP_WIDTH), F32)],
        compiler_params=pltpu.CompilerParams(
            dimension_semantics=("arbitrary",), vmem_limit_bytes=VMEM_LIMIT_BYTES),
        name="proj_gmlp",
    )(x2, g1, w_in, bd, qg, kg, lng, lnb, ws, bs, mg)


def _attn_kernel(q_ref, k_ref, v_ref, *rest, seq, residues, first, last):
    if first:
        o_in = lse_in = None
        o_out, lse_out = rest
    elif last:
        o_in, lse_in, o_out = rest
        lse_out = None
    else:
        o_in, lse_in, o_out, lse_out = rest
    n_blocks = seq // BLOCK

    lane = lax.broadcasted_iota(jnp.int32, (BLOCK, LANES), 1)
    low_half = lane < HEAD_DIM
    r2 = lax.broadcasted_iota(jnp.int32, (BLOCK, 2 * BLOCK), 0)
    c2 = lax.broadcasted_iota(jnp.int32, (BLOCK, 2 * BLOCK), 1)
    bias_band = jnp.where((c2 >= r2) & (c2 <= r2 + BLOCK), 0.0, -jnp.inf).astype(F32)
    r1 = lax.broadcasted_iota(jnp.int32, (BLOCK, BLOCK), 0)
    c1 = lax.broadcasted_iota(jnp.int32, (BLOCK, BLOCK), 1)
    bias_causal = jnp.where(c1 <= r1, 0.0, -jnp.inf).astype(F32)

    def block(r0, col_base, has_prev):
        rows = pl.ds(r0, BLOCK)
        key_rows = pl.ds(r0 - BLOCK, 2 * BLOCK) if has_prev else rows
        n_keys = 2 * BLOCK if has_prev else BLOCK
        bias = bias_band if has_prev else bias_causal
        ones = jnp.ones((n_keys, LANES), BF16)
        for p in range(LANE_GROUPS):
            cols = pl.ds(col_base + p * LANES, LANES)
            qt = q_ref[rows, cols]
            kk = k_ref[key_rows, cols]
            vv = jnp.concatenate([v_ref[key_rows, cols], ones], axis=1)
            accs, dens, maxs = [], [], []
            for hh in range(HEADS_PER_LANE_GROUP):
                keep = low_half if hh == 0 else jnp.logical_not(low_half)
                qm = jnp.where(keep, qt, jnp.zeros_like(qt))
                s = lax.dot_general(qm, kk, (((1,), (1,)), ((), ())),
                                    preferred_element_type=F32) + bias
                m = jnp.max(s, axis=-1, keepdims=True)
                pv = jnp.dot(jnp.exp(s - m).astype(BF16), vv, preferred_element_type=F32)
                accs.append(pv[:, :LANES])
                dens.append(pv[:, LANES:])
                maxs.append(m)
            acc = jnp.where(low_half, accs[0], accs[1])
            den = jnp.where(low_half, dens[0], dens[1])
            mx = jnp.where(low_half, maxs[0], maxs[1])
            if first:
                o_new = acc / den
                lse_new = mx + jnp.log(den)
            else:
                o_prev = o_in[rows, cols].astype(F32)
                lse_prev = lse_in[rows, cols]
                top = jnp.maximum(lse_prev, mx)
                w_prev = jnp.exp(lse_prev - top)
                w_new = jnp.exp(mx - top)
                total = w_prev + w_new * den
                o_new = (w_prev * o_prev + w_new * acc) / total
                lse_new = top + jnp.log(total)
            o_out[rows, cols] = o_new.astype(o_out.dtype)
            if lse_out is not None:
                lse_out[rows, cols] = lse_new

    for res in range(residues):
        col_base = res * ATTN_WIDTH
        block(0, col_base, False)
        if n_blocks > 1:
            def body(i, carry, col_base=col_base):
                block(pl.multiple_of(i * BLOCK, BLOCK), col_base, True)
                return carry
            lax.fori_loop(1, n_blocks, body, 0)


def _attn_call(q, k, v, state, *, dilation, first, last):
    batch, seq_full, width = q.shape
    seq = seq_full // dilation
    assert seq % BLOCK == 0 and width == ATTN_WIDTH
    view = (batch, seq, dilation * width)
    spec = pl.BlockSpec((None, seq, dilation * width), lambda b: (b, 0, 0))
    args = [q.reshape(view), k.reshape(view), v.reshape(view)]
    in_specs = [spec, spec, spec]
    if not first:
        args += [state[0].reshape(view), state[1].reshape(view)]
        in_specs += [spec, spec]
    out_shape = [jax.ShapeDtypeStruct(view, BF16)]
    out_specs = [spec]
    if not last:
        out_shape.append(jax.ShapeDtypeStruct(view, F32))
        out_specs.append(spec)
    outs = pl.pallas_call(
        functools.partial(_attn_kernel, seq=seq, residues=dilation, first=first, last=last),
        grid=(batch,),
        in_specs=in_specs,
        out_specs=out_specs,
        out_shape=out_shape,
        compiler_params=pltpu.CompilerParams(
            dimension_semantics=("arbitrary",), vmem_limit_bytes=VMEM_LIMIT_BYTES),
        name=f"attn_d{dilation}",
    )(*args)
    shape = (batch, seq_full, width)
    return tuple(o.reshape(shape) for o in outs)


def _mlp_kernel(x_ref, a_ref, m_ref, ag_ref, wo_ref, g2_ref, w1_ref, w2_ref, out_ref):
    a = a_ref[...].astype(F32)
    ams = jnp.mean(a * a, axis=-1, keepdims=True)
    a_n = (a * lax.rsqrt(ams + EPS) * ag_ref[...]).astype(BF16)
    x1 = x_ref[...]
    x1 = x1 + jnp.dot(a_n, wo_ref[:ATTN_WIDTH, :], preferred_element_type=F32)
    x1 = x1 + jnp.dot(m_ref[...], wo_ref[ATTN_WIDTH:, :], preferred_element_type=F32)
    ms = jnp.mean(x1 * x1, axis=-1, keepdims=True)
    h = (x1 * lax.rsqrt(ms + EPS) * g2_ref[...]).astype(BF16)
    out_ref[...] = x1
    for c in range(w1_ref.shape[1] // FF_CHUNK):
        cols = slice(c * FF_CHUNK, (c + 1) * FF_CHUNK)
        f = jnp.maximum(jnp.dot(h, w1_ref[:, cols], preferred_element_type=F32), 0.0)
        out_ref[...] += jnp.dot((f * f).astype(BF16), w2_ref[cols, :], preferred_element_type=F32)


def _mlp_call(x2, a2, m2, ag, w_out, g2, w_ff1, w_ff2):
    n_tok, d_model = x2.shape
    d_ff = w_ff1.shape[1]
    return pl.pallas_call(
        _mlp_kernel,
        grid=(n_tok // MLP_ROWS,),
        in_specs=[
            pl.BlockSpec((MLP_ROWS, d_model), lambda i: (i, 0)),
            pl.BlockSpec((MLP_ROWS, ATTN_WIDTH), lambda i: (i, 0)),
            pl.BlockSpec((MLP_ROWS, GMLP_WIDTH), lambda i: (i, 0)),
            _const_spec((1, ATTN_WIDTH)),
            _const_spec((ATTN_WIDTH + GMLP_WIDTH, d_model)),
            _const_spec((1, d_model)),
            _const_spec((d_model, d_ff)),
            _const_spec((d_ff, d_model)),
        ],
        out_specs=pl.BlockSpec((MLP_ROWS, d_model), lambda i: (i, 0)),
        out_shape=jax.ShapeDtypeStruct((n_tok, d_model), F32),
        compiler_params=pltpu.CompilerParams(
            dimension_semantics=("arbitrary",), vmem_limit_bytes=VMEM_LIMIT_BYTES),
        name="outproj_mlp",
    )(x2, a2, m2, ag, w_out, g2, w_ff1, w_ff2)


def kernel(x, norm1_g, w_in, q_norm_g, k_norm_g, ln_v_g, ln_v_b, w_spatial, b_spatial, attn_out_g,
           gmlp_out_g, w_out, norm2_g, w_ff1, w_ff2):
    batch, seq, d_model = x.shape
    depth = w_in.shape[0]
    head_id = jnp.arange(ATTN_WIDTH) // HEAD_DIM
    block_diag_mean = jnp.where(head_id[:, None] == head_id[None, :], 1.0 / HEAD_DIM, 0.0).astype(BF16)
    causal = jnp.tril(jnp.ones((CHUNK, CHUNK), dtype=bool))
    score_scale = 1.0 / math.sqrt(HEAD_DIM)
    for l in range(depth):
        x2 = x.reshape(batch * seq, d_model)
        qg = (jnp.tile(q_norm_g[l], ATTN_HEADS) * score_scale).reshape(1, ATTN_WIDTH)
        kg = jnp.tile(k_norm_g[l], ATTN_HEADS).reshape(1, ATTN_WIDTH)
        ws = jnp.where(causal[None], w_spatial[l], 0.0).astype(BF16)
        bs = jnp.repeat(b_spatial[l].T, GMLP_GROUP_DIM, axis=1)
        q2, k2, v2, m2 = _proj_call(
            x2, norm1_g[l].reshape(1, d_model), w_in[l].astype(BF16), block_diag_mean, qg, kg,
            ln_v_g[l].reshape(1, GMLP_WIDTH), ln_v_b[l].reshape(1, GMLP_WIDTH), ws, bs,
            gmlp_out_g[l].reshape(1, GMLP_WIDTH))
        q3, k3, v3 = (t.reshape(batch, seq, ATTN_WIDTH) for t in (q2, k2, v2))
        state = None
        for bi, (window, dilation) in enumerate(WINDOW_DILATIONS):
            assert window // dilation == BLOCK
            state = _attn_call(q3, k3, v3, state, dilation=dilation, first=bi == 0,
                               last=bi == len(WINDOW_DILATIONS) - 1)
        a2 = state[0].reshape(batch * seq, ATTN_WIDTH)
        x2 = _mlp_call(x2, a2, m2, attn_out_g[l].reshape(1, ATTN_WIDTH), w_out[l].astype(BF16),
                       norm2_g[l].reshape(1, d_model), w_ff1[l].astype(BF16), w_ff2[l].astype(BF16))
        x = x2.reshape(batch, seq, d_model)
    return x
```

```python
import functools
import math

import jax
import jax.numpy as jnp
from jax import lax
from jax.experimental import pallas as pl
from jax.experimental.pallas import tpu as pltpu

ATTN_HEADS = 8
HEAD_DIM = 64
ATTN_WIDTH = ATTN_HEADS * HEAD_DIM
GMLP_GROUPS = 4
GMLP_GROUP_DIM = 128
GMLP_WIDTH = GMLP_GROUPS * GMLP_GROUP_DIM
CHUNK = 128
BLOCK = 128
WINDOW_DILATIONS = ((128, 1), (512, 4), (2048, 16))
DILATIONS = tuple(d for _, d in WINDOW_DILATIONS)
HOP = 4
EPS = 1e-6

LANES = 128
HEADS_PER_LANE_GROUP = LANES // HEAD_DIM
LANE_GROUPS = ATTN_WIDTH // LANES
PROJ_ROWS = 512
MLP_ROWS = 512
FF_CHUNK = 1024
VMEM_LIMIT_BYTES = 56 * 1024 * 1024

F32 = jnp.float32
BF16 = jnp.bfloat16


def _const_spec(shape):
    return pl.BlockSpec(shape, lambda *_: (0,) * len(shape))


def _proj_kernel(x_ref, g1_ref, w_ref, bd_ref, qg_ref, kg_ref, lng_ref, lnb_ref, ws_ref, bs_ref,
                 mg_ref, qn, kn, vn, qm, km, vm, qw, kw, vw, m_out, m_scr, nat_scr, mid_scr):
    xf = x_ref[...]
    ms = jnp.mean(xf * xf, axis=-1, keepdims=True)
    h = (xf * lax.rsqrt(ms + EPS) * g1_ref[...]).astype(BF16)

    def proj(c0, n):
        return jnp.dot(h, w_ref[:, c0:c0 + n], preferred_element_type=F32)

    def head_norm(t, g_ref):
        msq = jnp.dot((t * t).astype(BF16), bd_ref[...], preferred_element_type=F32)
        return t * lax.rsqrt(msq + EPS) * g_ref[...]

    def emit(t_idx, val, out_n, out_m, out_w):
        sub = PROJ_ROWS // HOP
        subsub = sub // HOP
        for p in range(LANE_GROUPS):
            slab = t_idx * LANE_GROUPS + p
            part = val[:, p * LANES:(p + 1) * LANES]
            out_n[p] = part.astype(BF16)
            nat_scr[slab] = part
        for p in range(LANE_GROUPS):
            slab = t_idx * LANE_GROUPS + p
            for r in range(HOP):
                y = nat_scr[slab, pl.ds(r, sub, stride=HOP), :]
                out_m[r * LANE_GROUPS + p] = y.astype(BF16)
                mid_scr[slab, r * sub:(r + 1) * sub, :] = y
        for p in range(LANE_GROUPS):
            slab = t_idx * LANE_GROUPS + p
            for r in range(HOP):
                for a in range(HOP):
                    z = mid_scr[slab, pl.ds(r * sub + a, subsub, stride=HOP), :]
                    out_w[(a * HOP + r) * LANE_GROUPS + p] = z.astype(BF16)

    emit(0, head_norm(proj(0, ATTN_WIDTH), qg_ref), qn, qm, qw)
    emit(1, head_norm(proj(ATTN_WIDTH, ATTN_WIDTH), kg_ref), kn, km, kw)
    emit(2, proj(2 * ATTN_WIDTH, ATTN_WIDTH), vn, vm, vw)

    u = jax.nn.gelu(proj(3 * ATTN_WIDTH, GMLP_WIDTH))
    g = jax.nn.gelu(proj(3 * ATTN_WIDTH + GMLP_WIDTH, GMLP_WIDTH))
    for gi in range(GMLP_GROUPS):
        cols = slice(gi * GMLP_GROUP_DIM, (gi + 1) * GMLP_GROUP_DIM)
        gg = g[:, cols]
        mu = jnp.mean(gg, axis=-1, keepdims=True)
        xc = gg - mu
        var = jnp.mean(xc * xc, axis=-1, keepdims=True)
        y = (xc * lax.rsqrt(var + EPS) * lng_ref[:, cols] + lnb_ref[:, cols]).astype(BF16)
        for c in range(PROJ_ROWS // CHUNK):
            rows = slice(c * CHUNK, (c + 1) * CHUNK)
            z = jnp.dot(ws_ref[gi], y[rows, :], preferred_element_type=F32) + bs_ref[:, cols]
            m_scr[rows, cols] = u[rows, cols] * z
    mm = m_scr[...]
    mms = jnp.mean(mm * mm, axis=-1, keepdims=True)
    m_out[...] = (mm * lax.rsqrt(mms + EPS) * mg_ref[...]).astype(BF16)


def _layout_shape(batch, seq, dilation):
    return (batch, dilation * LANE_GROUPS, seq // dilation, LANES)


def _proj_call(x3, g1, w_in, bd, qg, kg, lng, lnb, ws, bs, mg):
    batch, seq, d_model = x3.shape
    in_width = w_in.shape[1]
    tiles = seq // PROJ_ROWS
    layout_shapes, layout_specs = [], []
    for dilation in DILATIONS:
        shape = _layout_shape(batch, seq, dilation)
        layout_shapes += [jax.ShapeDtypeStruct(shape, BF16)] * 3
        layout_specs += [pl.BlockSpec((None, shape[1], shape[2] // tiles, LANES),
                                      lambda b, j: (b, 0, j, 0))] * 3
    n_slabs = 3 * LANE_GROUPS
    return pl.pallas_call(
        _proj_kernel,
        grid=(batch, tiles),
        in_specs=[
            pl.BlockSpec((None, PROJ_ROWS, d_model), lambda b, j: (b, j, 0)),
            _const_spec((1, d_model)),
            _const_spec((d_model, in_width)),
            _const_spec((ATTN_WIDTH, ATTN_WIDTH)),
            _const_spec((1, ATTN_WIDTH)),
            _const_spec((1, ATTN_WIDTH)),
            _const_spec((1, GMLP_WIDTH)),
            _const_spec((1, GMLP_WIDTH)),
            _const_spec((GMLP_GROUPS, CHUNK, CHUNK)),
            _const_spec((CHUNK, GMLP_WIDTH)),
            _const_spec((1, GMLP_WIDTH)),
        ],
        out_specs=layout_specs + [pl.BlockSpec((None, PROJ_ROWS, GMLP_WIDTH), lambda b, j: (b, j, 0))],
        out_shape=layout_shapes + [jax.ShapeDtypeStruct((batch, seq, GMLP_WIDTH), BF16)],
        scratch_shapes=[pltpu.VMEM((PROJ_ROWS, GMLP_WIDTH), F32),
                        pltpu.VMEM((n_slabs, PROJ_ROWS, LANES), F32),
                        pltpu.VMEM((n_slabs, PROJ_ROWS, LANES), F32)],
        compiler_params=pltpu.CompilerParams(
            dimension_semantics=("arbitrary", "arbitrary"), vmem_limit_bytes=VMEM_LIMIT_BYTES),
        name="proj_gmlp",
    )(x3, g1, w_in, bd, qg, kg, lng, lnb, ws, bs, mg)


def _attn_kernel(q_ref, k_ref, v_ref, *rest, residues, n_blocks, first, last):
    if first:
        o_in = lse_in = None
        o_out, lse_out, o_scr = rest
    elif last:
        o_in, lse_in, o_out = rest
        lse_out = o_scr = None
    else:
        o_in, lse_in, o_out, lse_out, o_scr = rest
    next_residues = max(residues // HOP, 1)

    lane = lax.broadcasted_iota(jnp.int32, (BLOCK, LANES), 1)
    low_half = lane < HEAD_DIM
    high_half = jnp.logical_not(low_half)
    ones_low = jnp.where(low_half, 1.0, 0.0).astype(BF16)
    ones_high = jnp.where(high_half, 1.0, 0.0).astype(BF16)
    r2 = lax.broadcasted_iota(jnp.int32, (BLOCK, 2 * BLOCK), 0)
    c2 = lax.broadcasted_iota(jnp.int32, (BLOCK, 2 * BLOCK), 1)
    bias_band = jnp.where((c2 >= r2) & (c2 <= r2 + BLOCK), 0.0, -jnp.inf).astype(F32)
    r1 = lax.broadcasted_iota(jnp.int32, (BLOCK, BLOCK), 0)
    c1 = lax.broadcasted_iota(jnp.int32, (BLOCK, BLOCK), 1)
    bias_causal = jnp.where(c1 <= r1, 0.0, -jnp.inf).astype(F32)
    bias_causal2 = jnp.concatenate([bias_causal, bias_causal], axis=1)

    def pair_tile_first_block(slab, rows):
        qt = q_ref[slab, rows, :]
        kt = k_ref[slab, rows, :]
        vt = v_ref[slab, rows, :]
        zero = jnp.zeros_like(kt)
        k2 = jnp.concatenate([jnp.where(low_half, kt, zero), jnp.where(high_half, kt, zero)], axis=0)
        v2 = jnp.concatenate([
            jnp.concatenate([jnp.where(low_half, vt, zero), ones_low], axis=1),
            jnp.concatenate([jnp.where(high_half, vt, zero), ones_high], axis=1)], axis=0)
        s = lax.dot_general(qt, k2, (((1,), (1,)), ((), ())), preferred_element_type=F32) + bias_causal2
        s0, s1 = s[:, :BLOCK], s[:, BLOCK:]
        m0 = jnp.max(s0, axis=-1, keepdims=True)
        m1 = jnp.max(s1, axis=-1, keepdims=True)
        prob = jnp.concatenate([jnp.exp(s0 - m0), jnp.exp(s1 - m1)], axis=1).astype(BF16)
        pv = jnp.dot(prob, v2, preferred_element_type=F32)
        return pv[:, :LANES], pv[:, LANES:], jnp.where(low_half, m0, m1)

    def pair_tile(slab, r0):
        rows = pl.ds(r0, BLOCK)
        key_rows = pl.ds(r0 - BLOCK, 2 * BLOCK)
        qt = q_ref[slab, rows, :]
        kk = k_ref[slab, key_rows, :]
        vv = jnp.concatenate([v_ref[slab, key_rows, :], jnp.ones((2 * BLOCK, LANES), BF16)], axis=1)
        accs, dens, maxs = [], [], []
        for hh in range(HEADS_PER_LANE_GROUP):
            qh = jnp.where(low_half if hh == 0 else high_half, qt, jnp.zeros_like(qt))
            s = lax.dot_general(qh, kk, (((1,), (1,)), ((), ())), preferred_element_type=F32) + bias_band
            m = jnp.max(s, axis=-1, keepdims=True)
            pv = jnp.dot(jnp.exp(s - m).astype(BF16), vv, preferred_element_type=F32)
            accs.append(pv[:, :LANES])
            dens.append(pv[:, LANES:])
            maxs.append(m)
        return (jnp.where(low_half, accs[0], accs[1]), jnp.where(low_half, dens[0], dens[1]),
                jnp.where(low_half, maxs[0], maxs[1]))

    def block(res, r0, has_prev):
        rows = pl.ds(r0, BLOCK)
        for p in range(LANE_GROUPS):
            slab = res * LANE_GROUPS + p
            acc, den, mx = pair_tile(slab, r0) if has_prev else pair_tile_first_block(slab, rows)
            if first:
                o_new = acc / den
                lse_new = mx + jnp.log(den)
            else:
                o_prev = o_in[slab, rows, :].astype(F32)
                lse_prev = lse_in[slab, rows, :]
                top = jnp.maximum(lse_prev, mx)
                w_prev = jnp.exp(lse_prev - top)
                w_new = jnp.exp(mx - top)
                total = w_prev + w_new * den
                o_new = (w_prev * o_prev + w_new * acc) / total
                lse_new = top + jnp.log(total)
            if last:
                o_out[rows, p * LANES:(p + 1) * LANES] = o_new.astype(o_out.dtype)
            else:
                dst = (res % next_residues) * LANE_GROUPS + p
                dst_rows = pl.ds(HOP * r0 + res // next_residues, BLOCK, stride=HOP)
                o_scr[dst, dst_rows, :] = o_new
                lse_out[dst, dst_rows, :] = lse_new

    def residue(res):
        block(res, 0, False)
        if n_blocks > 1:
            def body(i, carry):
                block(res, pl.multiple_of(i * BLOCK, BLOCK), True)
                return carry
            lax.fori_loop(1, n_blocks, body, 0)

    if residues == 1:
        residue(0)
    else:
        def res_body(res, carry):
            residue(res)
            return carry
        lax.fori_loop(0, residues, res_body, 0)

    if not last:
        for s in range(o_scr.shape[0]):
            o_out[s] = o_scr[s].astype(o_out.dtype)


def _attn_call(q, k, v, state, *, first, last):
    batch, slabs, seq, _ = q.shape
    residues = slabs // LANE_GROUPS
    assert seq % BLOCK == 0
    spec = pl.BlockSpec((None, slabs, seq, LANES), lambda b: (b, 0, 0, 0))
    args = [q, k, v]
    in_specs = [spec, spec, spec]
    if not first:
        args += list(state)
        in_specs += [spec, spec]
    scratch = []
    if last:
        width = slabs * LANES
        out_shape = [jax.ShapeDtypeStruct((batch, seq, width), BF16)]
        out_specs = [pl.BlockSpec((None, seq, width), lambda b: (b, 0, 0))]
    else:
        nxt = (slabs // HOP, seq * HOP, LANES)
        nxt_spec = pl.BlockSpec((None,) + nxt, lambda b: (b, 0, 0, 0))
        out_shape = [jax.ShapeDtypeStruct((batch,) + nxt, BF16), jax.ShapeDtypeStruct((batch,) + nxt, F32)]
        out_specs = [nxt_spec, nxt_spec]
        scratch = [pltpu.VMEM(nxt, F32)]
    return pl.pallas_call(
        functools.partial(_attn_kernel, residues=residues, n_blocks=seq // BLOCK, first=first, last=last),
        grid=(batch,),
        in_specs=in_specs,
        out_specs=out_specs,
        out_shape=out_shape,
        scratch_shapes=scratch,
        compiler_params=pltpu.CompilerParams(
            dimension_semantics=("arbitrary",), vmem_limit_bytes=VMEM_LIMIT_BYTES),
        name=f"attn_d{residues}",
    )(*args)


def _mlp_kernel(x_ref, a_ref, m_ref, ag_ref, wo_ref, g2_ref, w1_ref, w2_ref, out_ref):
    a = a_ref[...].astype(F32)
    ams = jnp.mean(a * a, axis=-1, keepdims=True)
    a_n = (a * lax.rsqrt(ams + EPS) * ag_ref[...]).astype(BF16)
    x1 = x_ref[...]
    x1 = x1 + jnp.dot(a_n, wo_ref[:ATTN_WIDTH, :], preferred_element_type=F32)
    x1 = x1 + jnp.dot(m_ref[...], wo_ref[ATTN_WIDTH:, :], preferred_element_type=F32)
    ms = jnp.mean(x1 * x1, axis=-1, keepdims=True)
    h = (x1 * lax.rsqrt(ms + EPS) * g2_ref[...]).astype(BF16)
    out_ref[...] = x1
    for c in range(w1_ref.shape[1] // FF_CHUNK):
        cols = slice(c * FF_CHUNK, (c + 1) * FF_CHUNK)
        f = jnp.maximum(jnp.dot(h, w1_ref[:, cols], preferred_element_type=F32), 0.0)
        out_ref[...] += jnp.dot((f * f).astype(BF16), w2_ref[cols, :], preferred_element_type=F32)


def _mlp_call(x2, a2, m2, ag, w_out, g2, w_ff1, w_ff2):
    n_tok, d_model = x2.shape
    d_ff = w_ff1.shape[1]
    return pl.pallas_call(
        _mlp_kernel,
        grid=(n_tok // MLP_ROWS,),
        in_specs=[
            pl.BlockSpec((MLP_ROWS, d_model), lambda i: (i, 0)),
            pl.BlockSpec((MLP_ROWS, ATTN_WIDTH), lambda i: (i, 0)),
            pl.BlockSpec((MLP_ROWS, GMLP_WIDTH), lambda i: (i, 0)),
            _const_spec((1, ATTN_WIDTH)),
            _const_spec((ATTN_WIDTH + GMLP_WIDTH, d_model)),
            _const_spec((1, d_model)),
            _const_spec((d_model, d_ff)),
            _const_spec((d_ff, d_model)),
        ],
        out_specs=pl.BlockSpec((MLP_ROWS, d_model), lambda i: (i, 0)),
        out_shape=jax.ShapeDtypeStruct((n_tok, d_model), F32),
        compiler_params=pltpu.CompilerParams(
            dimension_semantics=("arbitrary",), vmem_limit_bytes=VMEM_LIMIT_BYTES),
        name="outproj_mlp",
    )(x2, a2, m2, ag, w_out, g2, w_ff1, w_ff2)


def kernel(x, norm1_g, w_in, q_norm_g, k_norm_g, ln_v_g, ln_v_b, w_spatial, b_spatial, attn_out_g,
           gmlp_out_g, w_out, norm2_g, w_ff1, w_ff2):
    batch, seq, d_model = x.shape
    depth = w_in.shape[0]
    assert DILATIONS == (1, HOP, HOP * HOP)
    assert all(window // dilation == BLOCK for window, dilation in WINDOW_DILATIONS)
    head_id = jnp.arange(ATTN_WIDTH) // HEAD_DIM
    block_diag_mean = jnp.where(head_id[:, None] == head_id[None, :], 1.0 / HEAD_DIM, 0.0).astype(BF16)
    causal = jnp.tril(jnp.ones((CHUNK, CHUNK), dtype=bool))
    score_scale = 1.0 / math.sqrt(HEAD_DIM)
    for l in range(depth):
        qg = (jnp.tile(q_norm_g[l], ATTN_HEADS) * score_scale).reshape(1, ATTN_WIDTH)
        kg = jnp.tile(k_norm_g[l], ATTN_HEADS).reshape(1, ATTN_WIDTH)
        ws = jnp.where(causal[None], w_spatial[l], 0.0).astype(BF16)
        bs = jnp.repeat(b_spatial[l].T, GMLP_GROUP_DIM, axis=1)
        *qkv, m3 = _proj_call(
            x, norm1_g[l].reshape(1, d_model), w_in[l].astype(BF16), block_diag_mean, qg, kg,
            ln_v_g[l].reshape(1, GMLP_WIDTH), ln_v_b[l].reshape(1, GMLP_WIDTH), ws, bs,
            gmlp_out_g[l].reshape(1, GMLP_WIDTH))
        state = None
        n_branches = len(DILATIONS)
        for bi in reversed(range(n_branches)):
            state = _attn_call(*qkv[3 * bi:3 * bi + 3], state, first=bi == n_branches - 1, last=bi == 0)
        x2 = _mlp_call(x.reshape(batch * seq, d_model), state[0].reshape(batch * seq, ATTN_WIDTH),
                       m3.reshape(batch * seq, GMLP_WIDTH), attn_out_g[l].reshape(1, ATTN_WIDTH),
                       w_out[l].astype(BF16), norm2_g[l].reshape(1, d_model), w_ff1[l].astype(BF16),
                       w_ff2[l].astype(BF16))
        x = x2.reshape(batch, seq, d_model)
    return x
```

```python
import functools
import math

import jax
import jax.numpy as jnp
from jax import lax
from jax.experimental import pallas as pl
from jax.experimental.pallas import tpu as pltpu

ATTN_HEADS = 8
HEAD_DIM = 64
ATTN_WIDTH = ATTN_HEADS * HEAD_DIM
GMLP_GROUPS = 4
GMLP_GROUP_DIM = 128
GMLP_WIDTH = GMLP_GROUPS * GMLP_GROUP_DIM
CHUNK = 128
BLOCK = 128
WINDOW_DILATIONS = ((128, 1), (512, 4), (2048, 16))
DILATIONS = tuple(d for _, d in WINDOW_DILATIONS)
HOP = 4
EPS = 1e-6

LANES = 128
MXU_WIDTH = 256
HEADS_PER_LANE_GROUP = LANES // HEAD_DIM
LANE_GROUPS = ATTN_WIDTH // LANES
PROJ_ROWS = 512
MLP_ROWS = 512
BLOCK_UNROLL = 5
RESIDUE_UNROLL = 4
FF_CHUNK = 1024
VMEM_LIMIT_BYTES = 56 * 1024 * 1024

F32 = jnp.float32
BF16 = jnp.bfloat16


def _const_spec(shape):
    return pl.BlockSpec(shape, lambda *_: (0,) * len(shape))


def _proj_kernel(x_ref, g1_ref, w_ref, bd_ref, qg_ref, kg_ref, lng_ref, lnb_ref, ws_ref, bs_ref,
                 mg_ref, qn, kn, vn, qm, km, vm, qw, kw, vw, m_out, m_scr, nat_scr, mid_scr):
    xf = x_ref[...]
    ms = jnp.mean(xf * xf, axis=-1, keepdims=True)
    h = (xf * lax.rsqrt(ms + EPS) * g1_ref[...]).astype(BF16)

    def proj(c0, n):
        return jnp.dot(h, w_ref[:, c0:c0 + n], preferred_element_type=F32)

    def head_norm(t, g_ref):
        sq = (t * t).astype(BF16)
        width = bd_ref.shape[0]
        msq = jnp.concatenate(
            [jnp.dot(sq[:, c:c + width], bd_ref[...], preferred_element_type=F32)
             for c in range(0, ATTN_WIDTH, width)], axis=1)
        return t * lax.rsqrt(msq + EPS) * g_ref[...]

    def emit(t_idx, val, out_n, out_m, out_w):
        sub = PROJ_ROWS // HOP
        subsub = sub // HOP
        for p in range(LANE_GROUPS):
            slab = t_idx * LANE_GROUPS + p
            part = val[:, p * LANES:(p + 1) * LANES]
            out_n[p] = part.astype(BF16)
            nat_scr[slab] = part
        for p in range(LANE_GROUPS):
            slab = t_idx * LANE_GROUPS + p
            for r in range(HOP):
                y = nat_scr[slab, pl.ds(r, sub, stride=HOP), :]
                out_m[r * LANE_GROUPS + p] = y.astype(BF16)
                mid_scr[slab, r * sub:(r + 1) * sub, :] = y
        for p in range(LANE_GROUPS):
            slab = t_idx * LANE_GROUPS + p
            for r in range(HOP):
                for a in range(HOP):
                    z = mid_scr[slab, pl.ds(r * sub + a, subsub, stride=HOP), :]
                    out_w[(a * HOP + r) * LANE_GROUPS + p] = z.astype(BF16)

    n_chunks = PROJ_ROWS // CHUNK
    u = jax.nn.gelu(proj(3 * ATTN_WIDTH, GMLP_WIDTH))
    g = jax.nn.gelu(proj(3 * ATTN_WIDTH + GMLP_WIDTH, GMLP_WIDTH))
    for gi in range(GMLP_GROUPS):
        cols = slice(gi * GMLP_GROUP_DIM, (gi + 1) * GMLP_GROUP_DIM)
        gg = g[:, cols]
        mu = jnp.mean(gg, axis=-1, keepdims=True)
        xc = gg - mu
        var = jnp.mean(xc * xc, axis=-1, keepdims=True)
        y = (xc * lax.rsqrt(var + EPS) * lng_ref[:, cols] + lnb_ref[:, cols]).astype(BF16)
        y_wide = jnp.concatenate([y[c * CHUNK:(c + 1) * CHUNK, :] for c in range(n_chunks)], axis=1)
        z_wide = jnp.dot(ws_ref[gi], y_wide, preferred_element_type=F32)
        for c in range(n_chunks):
            rows = slice(c * CHUNK, (c + 1) * CHUNK)
            z = z_wide[:, c * GMLP_GROUP_DIM:(c + 1) * GMLP_GROUP_DIM] + bs_ref[:, cols]
            m_scr[rows, cols] = u[rows, cols] * z
    mm = m_scr[...]
    mms = jnp.mean(mm * mm, axis=-1, keepdims=True)
    m_out[...] = (mm * lax.rsqrt(mms + EPS) * mg_ref[...]).astype(BF16)

    emit(0, head_norm(proj(0, ATTN_WIDTH), qg_ref), qn, qm, qw)
    emit(1, head_norm(proj(ATTN_WIDTH, ATTN_WIDTH), kg_ref), kn, km, kw)
    emit(2, proj(2 * ATTN_WIDTH, ATTN_WIDTH), vn, vm, vw)


def _layout_shape(batch, seq, dilation):
    return (batch, dilation * LANE_GROUPS, seq // dilation, LANES)


def _proj_call(x3, g1, w_in, bd, qg, kg, lng, lnb, ws, bs, mg):
    batch, seq, d_model = x3.shape
    in_width = w_in.shape[1]
    tiles = seq // PROJ_ROWS
    layout_shapes, layout_specs = [], []
    for dilation in DILATIONS:
        shape = _layout_shape(batch, seq, dilation)
        layout_shapes += [jax.ShapeDtypeStruct(shape, BF16)] * 3
        layout_specs += [pl.BlockSpec((None, shape[1], shape[2] // tiles, LANES),
                                      lambda b, j: (b, 0, j, 0))] * 3
    n_slabs = 3 * LANE_GROUPS
    return pl.pallas_call(
        _proj_kernel,
        grid=(batch, tiles),
        in_specs=[
            pl.BlockSpec((None, PROJ_ROWS, d_model), lambda b, j: (b, j, 0)),
            _const_spec((1, d_model)),
            _const_spec((d_model, in_width)),
            _const_spec(bd.shape),
            _const_spec((1, ATTN_WIDTH)),
            _const_spec((1, ATTN_WIDTH)),
            _const_spec((1, GMLP_WIDTH)),
            _const_spec((1, GMLP_WIDTH)),
            _const_spec((GMLP_GROUPS, CHUNK, CHUNK)),
            _const_spec((CHUNK, GMLP_WIDTH)),
            _const_spec((1, GMLP_WIDTH)),
        ],
        out_specs=layout_specs + [pl.BlockSpec((None, PROJ_ROWS, GMLP_WIDTH), lambda b, j: (b, j, 0))],
        out_shape=layout_shapes + [jax.ShapeDtypeStruct((batch, seq, GMLP_WIDTH), BF16)],
        scratch_shapes=[pltpu.VMEM((PROJ_ROWS, GMLP_WIDTH), F32),
                        pltpu.VMEM((n_slabs, PROJ_ROWS, LANES), F32),
                        pltpu.VMEM((n_slabs, PROJ_ROWS, LANES), F32)],
        compiler_params=pltpu.CompilerParams(
            dimension_semantics=("arbitrary", "arbitrary"), vmem_limit_bytes=VMEM_LIMIT_BYTES),
        name="proj_gmlp",
    )(x3, g1, w_in, bd, qg, kg, lng, lnb, ws, bs, mg)


def _attn_kernel(q_ref, k_ref, v_ref, *rest, residues, n_blocks, first, last):
    if first:
        o_in = lse_in = None
        o_out, lse_out, o_scr = rest
    elif last:
        o_in, lse_in, o_out = rest
        lse_out = o_scr = None
    else:
        o_in, lse_in, o_out, lse_out, o_scr = rest
    next_residues = max(residues // HOP, 1)

    lane = lax.broadcasted_iota(jnp.int32, (BLOCK, LANES), 1)
    low_half = lane < HEAD_DIM
    high_half = jnp.logical_not(low_half)
    ones_low = jnp.where(low_half, 1.0, 0.0).astype(BF16)
    ones_high = jnp.where(high_half, 1.0, 0.0).astype(BF16)
    r2 = lax.broadcasted_iota(jnp.int32, (BLOCK, 2 * BLOCK), 0)
    c2 = lax.broadcasted_iota(jnp.int32, (BLOCK, 2 * BLOCK), 1)
    bias_band = jnp.where((c2 >= r2) & (c2 <= r2 + BLOCK), 0.0, -jnp.inf).astype(F32)
    r1 = lax.broadcasted_iota(jnp.int32, (BLOCK, BLOCK), 0)
    c1 = lax.broadcasted_iota(jnp.int32, (BLOCK, BLOCK), 1)
    bias_causal = jnp.where(c1 <= r1, 0.0, -jnp.inf).astype(F32)
    bias_causal2 = jnp.concatenate([bias_causal, bias_causal], axis=1)

    def pair_tile_first_block(slab, rows):
        qt = q_ref[slab, rows, :]
        kt = k_ref[slab, rows, :]
        vt = v_ref[slab, rows, :]
        zero = jnp.zeros_like(kt)
        k2 = jnp.concatenate([jnp.where(low_half, kt, zero), jnp.where(high_half, kt, zero)], axis=0)
        v2 = jnp.concatenate([
            jnp.concatenate([jnp.where(low_half, vt, zero), ones_low], axis=1),
            jnp.concatenate([jnp.where(high_half, vt, zero), ones_high], axis=1)], axis=0)
        s = lax.dot_general(qt, k2, (((1,), (1,)), ((), ())), preferred_element_type=F32) + bias_causal2
        s0, s1 = s[:, :BLOCK], s[:, BLOCK:]
        m0 = jnp.max(s0, axis=-1, keepdims=True)
        m1 = jnp.max(s1, axis=-1, keepdims=True)
        prob = jnp.concatenate([jnp.exp2(s0 - m0), jnp.exp2(s1 - m1)], axis=1).astype(BF16)
        pv = jnp.dot(prob, v2, preferred_element_type=F32)
        return pv[:, :LANES], pv[:, LANES:], jnp.where(low_half, m0, m1)

    def pair_tile(slab, r0):
        rows = pl.ds(r0, BLOCK)
        key_rows = pl.ds(r0 - BLOCK, 2 * BLOCK)
        qt = q_ref[slab, rows, :]
        kk = k_ref[slab, key_rows, :]
        vv = jnp.concatenate([v_ref[slab, key_rows, :], jnp.ones((2 * BLOCK, LANES), BF16)], axis=1)
        accs, dens, maxs = [], [], []
        for hh in range(HEADS_PER_LANE_GROUP):
            qh = jnp.where(low_half if hh == 0 else high_half, qt, jnp.zeros_like(qt))
            s = lax.dot_general(qh, kk, (((1,), (1,)), ((), ())), preferred_element_type=F32) + bias_band
            m = jnp.max(s, axis=-1, keepdims=True)
            pv = jnp.dot(jnp.exp2(s - m).astype(BF16), vv, preferred_element_type=F32)
            accs.append(pv[:, :LANES])
            dens.append(pv[:, LANES:])
            maxs.append(m)
        return (jnp.where(low_half, accs[0], accs[1]), jnp.where(low_half, dens[0], dens[1]),
                jnp.where(low_half, maxs[0], maxs[1]))

    def block(res, r0, has_prev):
        rows = pl.ds(r0, BLOCK)
        for p in range(LANE_GROUPS):
            slab = res * LANE_GROUPS + p
            acc, den, mx = pair_tile(slab, r0) if has_prev else pair_tile_first_block(slab, rows)
            if first:
                o_new = acc / den
                lse_new = mx + jnp.log2(den)
            else:
                o_prev = o_in[slab, rows, :].astype(F32)
                lse_prev = lse_in[slab, rows, :]
                top = jnp.maximum(lse_prev, mx)
                w_prev = jnp.exp2(lse_prev - top)
                w_new = jnp.exp2(mx - top)
                total = w_prev + w_new * den
                o_new = (w_prev * o_prev + w_new * acc) / total
                lse_new = top + jnp.log2(total)
            if last:
                o_out[rows, p * LANES:(p + 1) * LANES] = o_new.astype(o_out.dtype)
            else:
                dst = (res % next_residues) * LANE_GROUPS + p
                dst_rows = pl.ds(HOP * r0 + res // next_residues, BLOCK, stride=HOP)
                o_scr[dst, dst_rows, :] = o_new
                lse_out[dst, dst_rows, :] = lse_new

    def residue(res):
        block(res, 0, False)
        if n_blocks > 1:
            def body(i, carry):
                block(res, pl.multiple_of(i * BLOCK, BLOCK), True)
                return carry
            lax.fori_loop(1, n_blocks, body, 0, unroll=BLOCK_UNROLL)

    if residues == 1:
        residue(0)
    else:
        def res_body(res, carry):
            residue(res)
            return carry
        lax.fori_loop(0, residues, res_body, 0, unroll=RESIDUE_UNROLL if n_blocks == 1 else 1)

    if not last:
        for s in range(o_scr.shape[0]):
            o_out[s] = o_scr[s].astype(o_out.dtype)


def _attn_call(q, k, v, state, *, first, last):
    batch, slabs, seq, _ = q.shape
    residues = slabs // LANE_GROUPS
    assert seq % BLOCK == 0
    spec = pl.BlockSpec((None, slabs, seq, LANES), lambda b: (b, 0, 0, 0))
    args = [q, k, v]
    in_specs = [spec, spec, spec]
    if not first:
        args += list(state)
        in_specs += [spec, spec]
    scratch = []
    if last:
        width = slabs * LANES
        out_shape = [jax.ShapeDtypeStruct((batch, seq, width), BF16)]
        out_specs = [pl.BlockSpec((None, seq, width), lambda b: (b, 0, 0))]
    else:
        nxt = (slabs // HOP, seq * HOP, LANES)
        nxt_spec = pl.BlockSpec((None,) + nxt, lambda b: (b, 0, 0, 0))
        out_shape = [jax.ShapeDtypeStruct((batch,) + nxt, BF16), jax.ShapeDtypeStruct((batch,) + nxt, F32)]
        out_specs = [nxt_spec, nxt_spec]
        scratch = [pltpu.VMEM(nxt, F32)]
    return pl.pallas_call(
        functools.partial(_attn_kernel, residues=residues, n_blocks=seq // BLOCK, first=first, last=last),
        grid=(batch,),
        in_specs=in_specs,
        out_specs=out_specs,
        out_shape=out_shape,
        scratch_shapes=scratch,
        compiler_params=pltpu.CompilerParams(
            dimension_semantics=("arbitrary",), vmem_limit_bytes=VMEM_LIMIT_BYTES),
        name=f"attn_d{residues}",
    )(*args)


def _mlp_kernel(x_ref, a_ref, m_ref, ag_ref, wo_ref, g2_ref, w1_ref, w2_ref, out_ref):
    a = a_ref[...].astype(F32)
    ams = jnp.mean(a * a, axis=-1, keepdims=True)
    a_n = (a * lax.rsqrt(ams + EPS) * ag_ref[...]).astype(BF16)
    x1 = x_ref[...]
    x1 = x1 + jnp.dot(a_n, wo_ref[:ATTN_WIDTH, :], preferred_element_type=F32)
    x1 = x1 + jnp.dot(m_ref[...], wo_ref[ATTN_WIDTH:, :], preferred_element_type=F32)
    ms = jnp.mean(x1 * x1, axis=-1, keepdims=True)
    h = (x1 * lax.rsqrt(ms + EPS) * g2_ref[...]).astype(BF16)
    out_ref[...] = x1
    for c in range(w1_ref.shape[1] // FF_CHUNK):
        cols = slice(c * FF_CHUNK, (c + 1) * FF_CHUNK)
        f = jnp.maximum(jnp.dot(h, w1_ref[:, cols], preferred_element_type=F32), 0.0)
        out_ref[...] += jnp.dot((f * f).astype(BF16), w2_ref[cols, :], preferred_element_type=F32)


def _mlp_call(x2, a2, m2, ag, w_out, g2, w_ff1, w_ff2):
    n_tok, d_model = x2.shape
    d_ff = w_ff1.shape[1]
    return pl.pallas_call(
        _mlp_kernel,
        grid=(n_tok // MLP_ROWS,),
        in_specs=[
            pl.BlockSpec((MLP_ROWS, d_model), lambda i: (i, 0)),
            pl.BlockSpec((MLP_ROWS, ATTN_WIDTH), lambda i: (i, 0)),
            pl.BlockSpec((MLP_ROWS, GMLP_WIDTH), lambda i: (i, 0)),
            _const_spec((1, ATTN_WIDTH)),
            _const_spec((ATTN_WIDTH + GMLP_WIDTH, d_model)),
            _const_spec((1, d_model)),
            _const_spec((d_model, d_ff)),
            _const_spec((d_ff, d_model)),
        ],
        out_specs=pl.BlockSpec((MLP_ROWS, d_model), lambda i: (i, 0)),
        out_shape=jax.ShapeDtypeStruct((n_tok, d_model), F32),
        compiler_params=pltpu.CompilerParams(
            dimension_semantics=("arbitrary",), vmem_limit_bytes=VMEM_LIMIT_BYTES),
        name="outproj_mlp",
    )(x2, a2, m2, ag, w_out, g2, w_ff1, w_ff2)


def kernel(x, norm1_g, w_in, q_norm_g, k_norm_g, ln_v_g, ln_v_b, w_spatial, b_spatial, attn_out_g,
           gmlp_out_g, w_out, norm2_g, w_ff1, w_ff2):
    batch, seq, d_model = x.shape
    depth = w_in.shape[0]
    assert DILATIONS == (1, HOP, HOP * HOP)
    assert all(window // dilation == BLOCK for window, dilation in WINDOW_DILATIONS)
    head_id = jnp.arange(MXU_WIDTH) // HEAD_DIM
    block_diag_mean = jnp.where(head_id[:, None] == head_id[None, :], 1.0 / HEAD_DIM, 0.0).astype(BF16)
    causal = jnp.tril(jnp.ones((CHUNK, CHUNK), dtype=bool))
    score_scale = math.log2(math.e) / math.sqrt(HEAD_DIM)
    for l in range(depth):
        qg = (jnp.tile(q_norm_g[l], ATTN_HEADS) * score_scale).reshape(1, ATTN_WIDTH)
        kg = jnp.tile(k_norm_g[l], ATTN_HEADS).reshape(1, ATTN_WIDTH)
        ws = jnp.where(causal[None], w_spatial[l], 0.0).astype(BF16)
        bs = jnp.repeat(b_spatial[l].T, GMLP_GROUP_DIM, axis=1)
        *qkv, m3 = _proj_call(
            x, norm1_g[l].reshape(1, d_model), w_in[l].astype(BF16), block_diag_mean, qg, kg,
            ln_v_g[l].reshape(1, GMLP_WIDTH), ln_v_b[l].reshape(1, GMLP_WIDTH), ws, bs,
            gmlp_out_g[l].reshape(1, GMLP_WIDTH))
        state = None
        n_branches = len(DILATIONS)
        for bi in reversed(range(n_branches)):
            state = _attn_call(*qkv[3 * bi:3 * bi + 3], state, first=bi == n_branches - 1, last=bi == 0)
        x2 = _mlp_call(x.reshape(batch * seq, d_model), state[0].reshape(batch * seq, ATTN_WIDTH),
                       m3.reshape(batch * seq, GMLP_WIDTH), attn_out_g[l].reshape(1, ATTN_WIDTH),
                       w_out[l].astype(BF16), norm2_g[l].reshape(1, d_model), w_ff1[l].astype(BF16),
                       w_ff2[l].astype(BF16))
        x = x2.reshape(batch, seq, d_model)
    return x
```

```python
import functools
import math

import jax
import jax.numpy as jnp
from jax import lax
from jax.experimental import pallas as pl
from jax.experimental.pallas import tpu as pltpu

ATTN_HEADS = 8
HEAD_DIM = 64
ATTN_WIDTH = ATTN_HEADS * HEAD_DIM
GMLP_GROUPS = 4
GMLP_GROUP_DIM = 128
GMLP_WIDTH = GMLP_GROUPS * GMLP_GROUP_DIM
CHUNK = 128
BLOCK = 128
WINDOW_DILATIONS = ((128, 1), (512, 4), (2048, 16))
DILATIONS = tuple(d for _, d in WINDOW_DILATIONS)
HOP = 4
EPS = 1e-6

LANES = 128
MXU_WIDTH = 256
HEADS_PER_LANE_GROUP = LANES // HEAD_DIM
LANE_GROUPS = ATTN_WIDTH // LANES
PROJ_ROWS = 512
MLP_ROWS = 512
BLOCK_UNROLL = 5
RESIDUE_UNROLL = 4
FF_CHUNK = 1024
VMEM_LIMIT_BYTES = 56 * 1024 * 1024

F32 = jnp.float32
BF16 = jnp.bfloat16


def _const_spec(shape):
    return pl.BlockSpec(shape, lambda *_: (0,) * len(shape))


def _proj_kernel(x_ref, g1_ref, w_ref, bd_ref, qg_ref, kg_ref,
                 qn, kn, vn, qm, km, vm, qw, kw, vw, ug_out, h_scr, raw_scr, nat_scr, mid_scr):
    step = pl.program_id(0)

    def normalise(h_out, half):
        rows = slice(half * (PROJ_ROWS // 2), (half + 1) * (PROJ_ROWS // 2))
        xf = x_ref[rows, :]
        ms = jnp.mean(xf * xf, axis=-1, keepdims=True)
        h_out[rows, :] = (xf * lax.rsqrt(ms + EPS) * g1_ref[...]).astype(BF16)

    def project(h_in, raw, c0):
        raw[:, c0:c0 + MXU_WIDTH] = jnp.dot(h_in[...], w_ref[:, c0:c0 + MXU_WIDTH],
                                            preferred_element_type=F32)

    def head_norm(t, g_ref):
        sq = (t * t).astype(BF16)
        width = bd_ref.shape[0]
        msq = jnp.concatenate(
            [jnp.dot(sq[:, c:c + width], bd_ref[...], preferred_element_type=F32)
             for c in range(0, ATTN_WIDTH, width)], axis=1)
        return t * lax.rsqrt(msq + EPS) * g_ref[...]

    def emit(t_idx, val, out_n, out_m, out_w):
        sub = PROJ_ROWS // HOP
        subsub = sub // HOP
        for p in range(LANE_GROUPS):
            slab = t_idx * LANE_GROUPS + p
            part = val[:, p * LANES:(p + 1) * LANES]
            out_n[p] = part.astype(BF16)
            nat_scr[slab] = part
        for p in range(LANE_GROUPS):
            slab = t_idx * LANE_GROUPS + p
            for r in range(HOP):
                y = nat_scr[slab, pl.ds(r, sub, stride=HOP), :]
                out_m[r * LANE_GROUPS + p] = y.astype(BF16)
                mid_scr[slab, r * sub:(r + 1) * sub, :] = y
        for p in range(LANE_GROUPS):
            slab = t_idx * LANE_GROUPS + p
            for r in range(HOP):
                for a in range(HOP):
                    z = mid_scr[slab, pl.ds(r * sub + a, subsub, stride=HOP), :]
                    out_w[(a * HOP + r) * LANE_GROUPS + p] = z.astype(BF16)

    gate_cols = 3 * ATTN_WIDTH

    def gate_raw(raw, half):
        cols = slice(half * GMLP_WIDTH, (half + 1) * GMLP_WIDTH)
        ug_out[:, cols] = raw[:, gate_cols + half * GMLP_WIDTH:gate_cols + (half + 1) * GMLP_WIDTH].astype(BF16)

    def finish_q(raw):
        emit(0, head_norm(raw[:, :ATTN_WIDTH], qg_ref), qn, qm, qw)

    def finish_k(raw):
        emit(1, head_norm(raw[:, ATTN_WIDTH:2 * ATTN_WIDTH], kg_ref), kn, km, kw)

    def finish_v(raw):
        emit(2, raw[:, 2 * ATTN_WIDTH:3 * ATTN_WIDTH], vn, vm, vw)

    def nothing(raw):
        del raw

    finish_parts = (finish_q, nothing, nothing, finish_k, nothing, nothing, finish_v,
                    functools.partial(gate_raw, half=0), functools.partial(gate_raw, half=1), nothing)

    def pipeline(cur, nxt):
        assert len(finish_parts) == w_ref.shape[1] // MXU_WIDTH
        for gi, part in enumerate(finish_parts):
            project(h_scr.at[cur], raw_scr.at[cur], gi * MXU_WIDTH)
            if gi < 2:
                normalise(h_scr.at[nxt], gi)
            part(raw_scr.at[nxt])

    @pl.when(step == 0)
    def _():
        h_scr[...] = jnp.zeros_like(h_scr)
        raw_scr[...] = jnp.zeros_like(raw_scr)

    @pl.when(step % 2 == 0)
    def _():
        pipeline(0, 1)

    @pl.when(step % 2 == 1)
    def _():
        pipeline(1, 0)


def _layout_shape(batch, seq, dilation):
    tiles = seq // PROJ_ROWS
    return (batch, tiles, dilation * LANE_GROUPS, seq // dilation // tiles, LANES)


def _proj_call(x3, g1, w_in, bd, qg, kg):
    batch, seq, d_model = x3.shape
    in_width = w_in.shape[1]
    tiles = seq // PROJ_ROWS
    n_tiles = batch * tiles

    def in_tile(i):
        t = jnp.minimum(i, n_tiles - 1)
        return t // tiles, t % tiles

    def out_tile(i):
        t = jnp.maximum(i - 2, 0)
        return t // tiles, t % tiles

    def layout_index(i):
        b, j = out_tile(i)
        return b, j, 0, 0, 0

    layout_shapes, layout_specs = [], []
    for dilation in DILATIONS:
        shape = _layout_shape(batch, seq, dilation)
        layout_shapes += [jax.ShapeDtypeStruct(shape, BF16)] * 3
        layout_specs += [pl.BlockSpec((None, None) + shape[2:], layout_index)] * 3
    n_slabs = 3 * LANE_GROUPS
    return pl.pallas_call(
        _proj_kernel,
        grid=(n_tiles + 2,),
        in_specs=[
            pl.BlockSpec((None, PROJ_ROWS, d_model), lambda i: in_tile(i) + (0,)),
            _const_spec((1, d_model)),
            _const_spec((d_model, in_width)),
            _const_spec(bd.shape),
            _const_spec((1, ATTN_WIDTH)),
            _const_spec((1, ATTN_WIDTH)),
        ],
        out_specs=layout_specs + [pl.BlockSpec((None, PROJ_ROWS, 2 * GMLP_WIDTH), lambda i: out_tile(i) + (0,))],
        out_shape=layout_shapes + [jax.ShapeDtypeStruct((batch, seq, 2 * GMLP_WIDTH), BF16)],
        scratch_shapes=[pltpu.VMEM((2, PROJ_ROWS, d_model), BF16),
                        pltpu.VMEM((2, PROJ_ROWS, in_width), F32),
                        pltpu.VMEM((n_slabs, PROJ_ROWS, LANES), F32),
                        pltpu.VMEM((n_slabs, PROJ_ROWS, LANES), F32)],
        compiler_params=pltpu.CompilerParams(
            dimension_semantics=("arbitrary",), vmem_limit_bytes=VMEM_LIMIT_BYTES),
        name="proj_gmlp",
    )(x3, g1, w_in, bd, qg, kg)


def _attn_kernel(q_ref, k_ref, v_ref, *rest, residues, n_blocks, first, last):
    if first:
        o_in = lse_in = None
        o_out, lse_out, o_scr = rest
    elif last:
        o_in, lse_in, o_out = rest
        lse_out = o_scr = None
    else:
        o_in, lse_in, o_out, lse_out, o_scr = rest
    next_residues = max(residues // HOP, 1)

    lane = lax.broadcasted_iota(jnp.int32, (BLOCK, LANES), 1)
    low_half = lane < HEAD_DIM
    high_half = jnp.logical_not(low_half)
    ones_low = jnp.where(low_half, 1.0, 0.0).astype(BF16)
    ones_high = jnp.where(high_half, 1.0, 0.0).astype(BF16)
    r2 = lax.broadcasted_iota(jnp.int32, (BLOCK, 2 * BLOCK), 0)
    c2 = lax.broadcasted_iota(jnp.int32, (BLOCK, 2 * BLOCK), 1)
    bias_band = jnp.where((c2 >= r2) & (c2 <= r2 + BLOCK), 0.0, -jnp.inf).astype(F32)
    r1 = lax.broadcasted_iota(jnp.int32, (BLOCK, BLOCK), 0)
    c1 = lax.broadcasted_iota(jnp.int32, (BLOCK, BLOCK), 1)
    bias_causal = jnp.where(c1 <= r1, 0.0, -jnp.inf).astype(F32)
    bias_causal2 = jnp.concatenate([bias_causal, bias_causal], axis=1)

    rows_per_tile = q_ref.shape[2]

    def load_block(ref, slab, blk):
        if rows_per_tile < BLOCK:
            assert n_blocks == 1 and rows_per_tile * ref.shape[0] == BLOCK
            return ref[:, slab, :, :].reshape(BLOCK, LANES)
        per_tile = rows_per_tile // BLOCK
        return ref[blk // per_tile, slab, pl.ds(pl.multiple_of((blk % per_tile) * BLOCK, BLOCK), BLOCK), :]

    def pair_tile_first_block(slab):
        qt = load_block(q_ref, slab, 0)
        kt = load_block(k_ref, slab, 0)
        vt = load_block(v_ref, slab, 0)
        zero = jnp.zeros_like(kt)
        k2 = jnp.concatenate([jnp.where(low_half, kt, zero), jnp.where(high_half, kt, zero)], axis=0)
        v2 = jnp.concatenate([
            jnp.concatenate([jnp.where(low_half, vt, zero), ones_low], axis=1),
            jnp.concatenate([jnp.where(high_half, vt, zero), ones_high], axis=1)], axis=0)
        s = lax.dot_general(qt, k2, (((1,), (1,)), ((), ())), preferred_element_type=F32) + bias_causal2
        s0, s1 = s[:, :BLOCK], s[:, BLOCK:]
        m0 = jnp.max(s0, axis=-1, keepdims=True)
        m1 = jnp.max(s1, axis=-1, keepdims=True)
        prob = jnp.concatenate([jnp.exp2(s0 - m0), jnp.exp2(s1 - m1)], axis=1).astype(BF16)
        pv = jnp.dot(prob, v2, preferred_element_type=F32)
        return pv[:, :LANES], pv[:, LANES:], jnp.where(low_half, m0, m1)

    def pair_tile(slab, blk):
        qt = load_block(q_ref, slab, blk)
        kk = jnp.concatenate([load_block(k_ref, slab, blk - 1), load_block(k_ref, slab, blk)], axis=0)
        vv = jnp.concatenate([
            jnp.concatenate([load_block(v_ref, slab, blk - 1), load_block(v_ref, slab, blk)], axis=0),
            jnp.ones((2 * BLOCK, LANES), BF16)], axis=1)
        accs, dens, maxs = [], [], []
        for hh in range(HEADS_PER_LANE_GROUP):
            qh = jnp.where(low_half if hh == 0 else high_half, qt, jnp.zeros_like(qt))
            s = lax.dot_general(qh, kk, (((1,), (1,)), ((), ())), preferred_element_type=F32) + bias_band
            m = jnp.max(s, axis=-1, keepdims=True)
            pv = jnp.dot(jnp.exp2(s - m).astype(BF16), vv, preferred_element_type=F32)
            accs.append(pv[:, :LANES])
            dens.append(pv[:, LANES:])
            maxs.append(m)
        return (jnp.where(low_half, accs[0], accs[1]), jnp.where(low_half, dens[0], dens[1]),
                jnp.where(low_half, maxs[0], maxs[1]))

    def block(res, blk, has_prev):
        r0 = blk * BLOCK if isinstance(blk, int) else pl.multiple_of(blk * BLOCK, BLOCK)
        rows = pl.ds(r0, BLOCK)
        for p in range(LANE_GROUPS):
            slab = res * LANE_GROUPS + p
            acc, den, mx = pair_tile(slab, blk) if has_prev else pair_tile_first_block(slab)
            if first:
                o_new = acc / den
                lse_new = mx + jnp.log2(den)
            else:
                o_prev = o_in[slab, rows, :].astype(F32)
                lse_prev = lse_in[slab, rows, :]
                top = jnp.maximum(lse_prev, mx)
                w_prev = jnp.exp2(lse_prev - top)
                w_new = jnp.exp2(mx - top)
                total = w_prev + w_new * den
                o_new = (w_prev * o_prev + w_new * acc) / total
                lse_new = top + jnp.log2(total)
            if last:
                o_out[rows, p * LANES:(p + 1) * LANES] = o_new.astype(o_out.dtype)
            else:
                dst = (res % next_residues) * LANE_GROUPS + p
                dst_rows = pl.ds(HOP * r0 + res // next_residues, BLOCK, stride=HOP)
                o_scr[dst, dst_rows, :] = o_new
                lse_out[dst, dst_rows, :] = lse_new

    def residue(res):
        block(res, 0, False)
        if n_blocks > 1:
            def body(i, carry):
                block(res, i, True)
                return carry
            lax.fori_loop(1, n_blocks, body, 0, unroll=BLOCK_UNROLL)

    if residues == 1:
        residue(0)
    else:
        def res_body(res, carry):
            residue(res)
            return carry
        lax.fori_loop(0, residues, res_body, 0, unroll=RESIDUE_UNROLL if n_blocks == 1 else 1)

    if not last:
        for s in range(o_scr.shape[0]):
            o_out[s] = o_scr[s].astype(o_out.dtype)


def _attn_call(q, k, v, state, *, first, last):
    batch, tiles, slabs, rows_per_tile, _ = q.shape
    residues = slabs // LANE_GROUPS
    seq = tiles * rows_per_tile
    assert seq % BLOCK == 0
    qkv_spec = pl.BlockSpec((None,) + q.shape[1:], lambda b: (b, 0, 0, 0, 0))
    spec = pl.BlockSpec((None, slabs, seq, LANES), lambda b: (b, 0, 0, 0))
    args = [q, k, v]
    in_specs = [qkv_spec, qkv_spec, qkv_spec]
    if not first:
        args += list(state)
        in_specs += [spec, spec]
    scratch = []
    if last:
        width = slabs * LANES
        out_shape = [jax.ShapeDtypeStruct((batch, seq, width), BF16)]
        out_specs = [pl.BlockSpec((None, seq, width), lambda b: (b, 0, 0))]
    else:
        nxt = (slabs // HOP, seq * HOP, LANES)
        nxt_spec = pl.BlockSpec((None,) + nxt, lambda b: (b, 0, 0, 0))
        out_shape = [jax.ShapeDtypeStruct((batch,) + nxt, BF16), jax.ShapeDtypeStruct((batch,) + nxt, F32)]
        out_specs = [nxt_spec, nxt_spec]
        scratch = [pltpu.VMEM(nxt, F32)]
    return pl.pallas_call(
        functools.partial(_attn_kernel, residues=residues, n_blocks=seq // BLOCK, first=first, last=last),
        grid=(batch,),
        in_specs=in_specs,
        out_specs=out_specs,
        out_shape=out_shape,
        scratch_shapes=scratch,
        compiler_params=pltpu.CompilerParams(
            dimension_semantics=("arbitrary",), vmem_limit_bytes=VMEM_LIMIT_BYTES),
        name=f"attn_d{residues}",
    )(*args)


def _mlp_kernel(x_ref, a_ref, ug_first, ug_next, lng_ref, lnb_ref, ws_ref, bs_ref, mg_ref, ag_ref,
                wo_ref, g2_ref, w1_ref, w2_ref, out_ref, gate_scr, m_scr):
    step = pl.program_id(0)
    n_chunks = MLP_ROWS // CHUNK

    def gate_group(ug_ref, gi):
        cols = slice(gi * GMLP_GROUP_DIM, (gi + 1) * GMLP_GROUP_DIM)
        u = jax.nn.gelu(ug_ref[:, gi * GMLP_GROUP_DIM:(gi + 1) * GMLP_GROUP_DIM].astype(F32))
        gg = jax.nn.gelu(ug_ref[:, GMLP_WIDTH + gi * GMLP_GROUP_DIM:
                                GMLP_WIDTH + (gi + 1) * GMLP_GROUP_DIM].astype(F32))
        mu = jnp.mean(gg, axis=-1, keepdims=True)
        xc = gg - mu
        var = jnp.mean(xc * xc, axis=-1, keepdims=True)
        y = (xc * lax.rsqrt(var + EPS) * lng_ref[:, cols] + lnb_ref[:, cols]).astype(BF16)
        y_wide = jnp.concatenate([y[c * CHUNK:(c + 1) * CHUNK, :] for c in range(n_chunks)], axis=1)
        z_wide = jnp.dot(ws_ref[gi], y_wide, preferred_element_type=F32)
        for c in range(n_chunks):
            rows = slice(c * CHUNK, (c + 1) * CHUNK)
            z = z_wide[:, c * GMLP_GROUP_DIM:(c + 1) * GMLP_GROUP_DIM] + bs_ref[:, cols]
            gate_scr[rows, cols] = u[rows, :] * z

    def gate_norm(slot):
        mm = gate_scr[...]
        mms = jnp.mean(mm * mm, axis=-1, keepdims=True)
        m_scr[slot] = (mm * lax.rsqrt(mms + EPS) * mg_ref[...]).astype(BF16)

    @pl.when(step == 0)
    def _():
        for gi in range(GMLP_GROUPS):
            gate_group(ug_first, gi)
        gate_norm(0)

    cur = step % 2
    a = a_ref[...].astype(F32)
    ams = jnp.mean(a * a, axis=-1, keepdims=True)
    a_n = (a * lax.rsqrt(ams + EPS) * ag_ref[...]).astype(BF16)
    x1 = x_ref[...]
    x1 = x1 + jnp.dot(a_n, wo_ref[:ATTN_WIDTH, :], preferred_element_type=F32)
    x1 = x1 + jnp.dot(m_scr[cur], wo_ref[ATTN_WIDTH:, :], preferred_element_type=F32)
    ms = jnp.mean(x1 * x1, axis=-1, keepdims=True)
    h = (x1 * lax.rsqrt(ms + EPS) * g2_ref[...]).astype(BF16)
    out_ref[...] = x1
    n_ff = w1_ref.shape[1] // FF_CHUNK
    assert n_ff == GMLP_GROUPS
    for c in range(n_ff):
        cols = slice(c * FF_CHUNK, (c + 1) * FF_CHUNK)
        f = jnp.maximum(jnp.dot(h, w1_ref[:, cols], preferred_element_type=F32), 0.0)
        gate_group(ug_next, c)
        if c == n_ff - 1:
            gate_norm(1 - cur)
        out_ref[...] += jnp.dot((f * f).astype(BF16), w2_ref[cols, :], preferred_element_type=F32)


def _mlp_call(x2, a2, ug2, lng, lnb, ws, bs, mg, ag, w_out, g2, w_ff1, w_ff2):
    n_tok, d_model = x2.shape
    d_ff = w_ff1.shape[1]
    n_tiles = n_tok // MLP_ROWS
    return pl.pallas_call(
        _mlp_kernel,
        grid=(n_tiles,),
        in_specs=[
            pl.BlockSpec((MLP_ROWS, d_model), lambda i: (i, 0)),
            pl.BlockSpec((MLP_ROWS, ATTN_WIDTH), lambda i: (i, 0)),
            pl.BlockSpec((MLP_ROWS, 2 * GMLP_WIDTH), lambda i: (0, 0)),
            pl.BlockSpec((MLP_ROWS, 2 * GMLP_WIDTH), lambda i: (jnp.minimum(i + 1, n_tiles - 1), 0)),
            _const_spec((1, GMLP_WIDTH)),
            _const_spec((1, GMLP_WIDTH)),
            _const_spec((GMLP_GROUPS, CHUNK, CHUNK)),
            _const_spec((CHUNK, GMLP_WIDTH)),
            _const_spec((1, GMLP_WIDTH)),
            _const_spec((1, ATTN_WIDTH)),
            _const_spec((ATTN_WIDTH + GMLP_WIDTH, d_model)),
            _const_spec((1, d_model)),
            _const_spec((d_model, d_ff)),
            _const_spec((d_ff, d_model)),
        ],
        out_specs=pl.BlockSpec((MLP_ROWS, d_model), lambda i: (i, 0)),
        out_shape=jax.ShapeDtypeStruct((n_tok, d_model), F32),
        scratch_shapes=[pltpu.VMEM((MLP_ROWS, GMLP_WIDTH), F32),
                        pltpu.VMEM((2, MLP_ROWS, GMLP_WIDTH), BF16)],
        compiler_params=pltpu.CompilerParams(
            dimension_semantics=("arbitrary",), vmem_limit_bytes=VMEM_LIMIT_BYTES),
        name="outproj_mlp",
    )(x2, a2, ug2, ug2, lng, lnb, ws, bs, mg, ag, w_out, g2, w_ff1, w_ff2)


def kernel(x, norm1_g, w_in, q_norm_g, k_norm_g, ln_v_g, ln_v_b, w_spatial, b_spatial, attn_out_g,
           gmlp_out_g, w_out, norm2_g, w_ff1, w_ff2):
    batch, seq, d_model = x.shape
    depth = w_in.shape[0]
    assert DILATIONS == (1, HOP, HOP * HOP)
    assert all(window // dilation == BLOCK for window, dilation in WINDOW_DILATIONS)
    head_id = jnp.arange(MXU_WIDTH) // HEAD_DIM
    block_diag_mean = jnp.where(head_id[:, None] == head_id[None, :], 1.0 / HEAD_DIM, 0.0).astype(BF16)
    causal = jnp.tril(jnp.ones((CHUNK, CHUNK), dtype=bool))
    score_scale = math.log2(math.e) / math.sqrt(HEAD_DIM)
    for l in range(depth):
        qg = (jnp.tile(q_norm_g[l], ATTN_HEADS) * score_scale).reshape(1, ATTN_WIDTH)
        kg = jnp.tile(k_norm_g[l], ATTN_HEADS).reshape(1, ATTN_WIDTH)
        ws = jnp.where(causal[None], w_spatial[l], 0.0).astype(BF16)
        bs = jnp.repeat(b_spatial[l].T, GMLP_GROUP_DIM, axis=1)
        *qkv, ug3 = _proj_call(x, norm1_g[l].reshape(1, d_model), w_in[l].astype(BF16), block_diag_mean, qg, kg)
        state = None
        n_branches = len(DILATIONS)
        for bi in reversed(range(n_branches)):
            state = _attn_call(*qkv[3 * bi:3 * bi + 3], state, first=bi == n_branches - 1, last=bi == 0)
        x2 = _mlp_call(x.reshape(batch * seq, d_model), state[0].reshape(batch * seq, ATTN_WIDTH),
                       ug3.reshape(batch * seq, 2 * GMLP_WIDTH), ln_v_g[l].reshape(1, GMLP_WIDTH),
                       ln_v_b[l].reshape(1, GMLP_WIDTH), ws, bs, gmlp_out_g[l].reshape(1, GMLP_WIDTH),
                       attn_out_g[l].reshape(1, ATTN_WIDTH), w_out[l].astype(BF16),
                       norm2_g[l].reshape(1, d_model), w_ff1[l].astype(BF16), w_ff2[l].astype(BF16))
        x = x2.reshape(batch, seq, d_model)
    return x
```

```python
import functools
import math

import jax
import jax.numpy as jnp
from jax import lax
from jax.experimental import pallas as pl
from jax.experimental.pallas import tpu as pltpu

ATTN_HEADS = 8
HEAD_DIM = 64
ATTN_WIDTH = ATTN_HEADS * HEAD_DIM
GMLP_GROUPS = 4
GMLP_GROUP_DIM = 128
GMLP_WIDTH = GMLP_GROUPS * GMLP_GROUP_DIM
CHUNK = 128
BLOCK = 128
WINDOW_DILATIONS = ((128, 1), (512, 4), (2048, 16))
DILATIONS = tuple(d for _, d in WINDOW_DILATIONS)
HOP = 4
EPS = 1e-6

LANES = 128
MXU_WIDTH = 256
HEADS_PER_LANE_GROUP = LANES // HEAD_DIM
LANE_GROUPS = ATTN_WIDTH // LANES
PROJ_ROWS = 512
MLP_ROWS = 512
BLOCK_UNROLL = 15
RESIDUE_UNROLL = 4
MULTI_BLOCK_RESIDUE_UNROLL = 2
FF_CHUNK = 1024
VMEM_LIMIT_BYTES = 56 * 1024 * 1024

F32 = jnp.float32
BF16 = jnp.bfloat16


def _const_spec(shape):
    return pl.BlockSpec(shape, lambda *_: (0,) * len(shape))


def _proj_kernel(x_ref, g1_ref, w_ref, bd_ref, qg_ref, kg_ref,
                 qn, kn, vn, qm, km, vm, qw, kw, vw, ug_out, h_scr, raw_scr, nat_scr, mid_scr):
    step = pl.program_id(0)

    def normalise(h_out, half):
        rows = slice(half * (PROJ_ROWS // 2), (half + 1) * (PROJ_ROWS // 2))
        xf = x_ref[rows, :]
        ms = jnp.mean(xf * xf, axis=-1, keepdims=True)
        h_out[rows, :] = (xf * lax.rsqrt(ms + EPS) * g1_ref[...]).astype(BF16)

    def project(h_in, raw, c0):
        raw[:, c0:c0 + MXU_WIDTH] = jnp.dot(h_in[...], w_ref[:, c0:c0 + MXU_WIDTH],
                                            preferred_element_type=F32)

    def head_norm(t, g_ref):
        sq = (t * t).astype(BF16)
        width = bd_ref.shape[0]
        msq = jnp.concatenate(
            [jnp.dot(sq[:, c:c + width], bd_ref[...], preferred_element_type=F32)
             for c in range(0, ATTN_WIDTH, width)], axis=1)
        return t * lax.rsqrt(msq + EPS) * g_ref[...]

    def emit(t_idx, val, out_n, out_m, out_w):
        sub = PROJ_ROWS // HOP
        subsub = sub // HOP
        for p in range(LANE_GROUPS):
            slab = t_idx * LANE_GROUPS + p
            part = val[:, p * LANES:(p + 1) * LANES]
            out_n[p] = part.astype(BF16)
            nat_scr[slab] = part
        for p in range(LANE_GROUPS):
            slab = t_idx * LANE_GROUPS + p
            for r in range(HOP):
                y = nat_scr[slab, pl.ds(r, sub, stride=HOP), :]
                out_m[r * LANE_GROUPS + p] = y.astype(BF16)
                mid_scr[slab, r * sub:(r + 1) * sub, :] = y
        for p in range(LANE_GROUPS):
            slab = t_idx * LANE_GROUPS + p
            for r in range(HOP):
                for a in range(HOP):
                    z = mid_scr[slab, pl.ds(r * sub + a, subsub, stride=HOP), :]
                    out_w[(a * HOP + r) * LANE_GROUPS + p] = z.astype(BF16)

    gate_cols = 3 * ATTN_WIDTH

    def gate_raw(raw, half):
        cols = slice(half * GMLP_WIDTH, (half + 1) * GMLP_WIDTH)
        ug_out[:, cols] = raw[:, gate_cols + half * GMLP_WIDTH:gate_cols + (half + 1) * GMLP_WIDTH].astype(BF16)

    def finish_q(raw):
        emit(0, head_norm(raw[:, :ATTN_WIDTH], qg_ref), qn, qm, qw)

    def finish_k(raw):
        emit(1, head_norm(raw[:, ATTN_WIDTH:2 * ATTN_WIDTH], kg_ref), kn, km, kw)

    def finish_v(raw):
        emit(2, raw[:, 2 * ATTN_WIDTH:3 * ATTN_WIDTH], vn, vm, vw)

    def nothing(raw):
        del raw

    finish_parts = (finish_q, nothing, nothing, finish_k, nothing, nothing, finish_v,
                    functools.partial(gate_raw, half=0), functools.partial(gate_raw, half=1), nothing)

    def pipeline(cur, nxt):
        assert len(finish_parts) == w_ref.shape[1] // MXU_WIDTH
        for gi, part in enumerate(finish_parts):
            project(h_scr.at[cur], raw_scr.at[cur], gi * MXU_WIDTH)
            if gi < 2:
                normalise(h_scr.at[nxt], gi)
            part(raw_scr.at[nxt])

    @pl.when(step == 0)
    def _():
        h_scr[...] = jnp.zeros_like(h_scr)
        raw_scr[...] = jnp.zeros_like(raw_scr)

    @pl.when(step % 2 == 0)
    def _():
        pipeline(0, 1)

    @pl.when(step % 2 == 1)
    def _():
        pipeline(1, 0)


def _layout_shape(batch, seq, dilation):
    tiles = seq // PROJ_ROWS
    return (batch, tiles, dilation * LANE_GROUPS, seq // dilation // tiles, LANES)


def _proj_call(x3, g1, w_in, bd, qg, kg):
    batch, seq, d_model = x3.shape
    in_width = w_in.shape[1]
    tiles = seq // PROJ_ROWS
    n_tiles = batch * tiles

    def in_tile(i):
        t = jnp.minimum(i, n_tiles - 1)
        return t // tiles, t % tiles

    def out_tile(i):
        t = jnp.maximum(i - 2, 0)
        return t // tiles, t % tiles

    def layout_index(i):
        b, j = out_tile(i)
        return b, j, 0, 0, 0

    layout_shapes, layout_specs = [], []
    for dilation in DILATIONS:
        shape = _layout_shape(batch, seq, dilation)
        layout_shapes += [jax.ShapeDtypeStruct(shape, BF16)] * 3
        layout_specs += [pl.BlockSpec((None, None) + shape[2:], layout_index)] * 3
    n_slabs = 3 * LANE_GROUPS
    return pl.pallas_call(
        _proj_kernel,
        grid=(n_tiles + 2,),
        in_specs=[
            pl.BlockSpec((None, PROJ_ROWS, d_model), lambda i: in_tile(i) + (0,)),
            _const_spec((1, d_model)),
            _const_spec((d_model, in_width)),
            _const_spec(bd.shape),
            _const_spec((1, ATTN_WIDTH)),
            _const_spec((1, ATTN_WIDTH)),
        ],
        out_specs=layout_specs + [pl.BlockSpec((None, PROJ_ROWS, 2 * GMLP_WIDTH), lambda i: out_tile(i) + (0,))],
        out_shape=layout_shapes + [jax.ShapeDtypeStruct((batch, seq, 2 * GMLP_WIDTH), BF16)],
        scratch_shapes=[pltpu.VMEM((2, PROJ_ROWS, d_model), BF16),
                        pltpu.VMEM((2, PROJ_ROWS, in_width), F32),
                        pltpu.VMEM((n_slabs, PROJ_ROWS, LANES), F32),
                        pltpu.VMEM((n_slabs, PROJ_ROWS, LANES), F32)],
        compiler_params=pltpu.CompilerParams(
            dimension_semantics=("arbitrary",), vmem_limit_bytes=VMEM_LIMIT_BYTES),
        name="proj_gmlp",
    )(x3, g1, w_in, bd, qg, kg)


def _attn_kernel(q_ref, k_ref, v_ref, *rest, residues, n_blocks, first, last):
    if first:
        o_in = lse_in = None
        o_out, lse_out, o_scr = rest
    elif last:
        o_in, lse_in, o_out = rest
        lse_out = o_scr = None
    else:
        o_in, lse_in, o_out, lse_out, o_scr = rest
    next_residues = max(residues // HOP, 1)

    low_half = lax.broadcasted_iota(jnp.int32, (BLOCK, LANES), 1) < HEAD_DIM

    def head_lanes(n_keys):
        low = lax.broadcasted_iota(jnp.int32, (n_keys, LANES), 1) < HEAD_DIM
        high = jnp.logical_not(low)
        return low, high, jnp.where(low, 1.0, 0.0).astype(BF16), jnp.where(high, 1.0, 0.0).astype(BF16)

    r2 = lax.broadcasted_iota(jnp.int32, (BLOCK, 2 * BLOCK), 0)
    c2 = lax.broadcasted_iota(jnp.int32, (BLOCK, 2 * BLOCK), 1)
    r1 = lax.broadcasted_iota(jnp.int32, (BLOCK, BLOCK), 0)
    c1 = lax.broadcasted_iota(jnp.int32, (BLOCK, BLOCK), 1)
    tile_consts = {
        True: (jnp.where((c2 >= r2) & (c2 <= r2 + BLOCK), 0.0, -jnp.inf).astype(F32),) + head_lanes(2 * BLOCK),
        False: (jnp.where(c1 <= r1, 0.0, -jnp.inf).astype(F32),) + head_lanes(BLOCK),
    }

    rows_per_tile = q_ref.shape[2]

    def load_block(ref, slab, blk):
        if rows_per_tile < BLOCK:
            assert n_blocks == 1 and rows_per_tile * ref.shape[0] == BLOCK
            return ref[:, slab, :, :].reshape(BLOCK, LANES)
        per_tile = rows_per_tile // BLOCK
        return ref[blk // per_tile, slab, pl.ds(pl.multiple_of((blk % per_tile) * BLOCK, BLOCK), BLOCK), :]

    def pair_tile(slab, blk, has_prev):
        bias, low, high, ones_low, ones_high = tile_consts[has_prev]
        qt = load_block(q_ref, slab, blk)
        kk = load_block(k_ref, slab, blk)
        vv = load_block(v_ref, slab, blk)
        if has_prev:
            kk = jnp.concatenate([load_block(k_ref, slab, blk - 1), kk], axis=0)
            vv = jnp.concatenate([load_block(v_ref, slab, blk - 1), vv], axis=0)
        n_keys = kk.shape[0]
        zero = jnp.zeros_like(kk)
        k2 = jnp.concatenate([jnp.where(low, kk, zero), jnp.where(high, kk, zero)], axis=0)
        v2 = jnp.concatenate([
            jnp.concatenate([jnp.where(low, vv, zero), ones_low], axis=1),
            jnp.concatenate([jnp.where(high, vv, zero), ones_high], axis=1)], axis=0)
        s = lax.dot_general(qt, k2, (((1,), (1,)), ((), ())), preferred_element_type=F32)
        s0 = s[:, :n_keys] + bias
        s1 = s[:, n_keys:] + bias
        m0 = jnp.max(s0, axis=-1, keepdims=True)
        m1 = jnp.max(s1, axis=-1, keepdims=True)
        prob = jnp.concatenate([jnp.exp2(s0 - m0), jnp.exp2(s1 - m1)], axis=1).astype(BF16)
        pv = jnp.dot(prob, v2, preferred_element_type=F32)
        return pv[:, :LANES], pv[:, LANES:], jnp.where(low_half, m0, m1)

    def block(res, blk, has_prev):
        r0 = blk * BLOCK if isinstance(blk, int) else pl.multiple_of(blk * BLOCK, BLOCK)
        rows = pl.ds(r0, BLOCK)
        for p in range(LANE_GROUPS):
            slab = res * LANE_GROUPS + p
            acc, den, mx = pair_tile(slab, blk, has_prev)
            if first:
                o_new = acc / den
                lse_new = mx + jnp.log2(den)
            else:
                o_prev = o_in[slab, rows, :].astype(F32)
                lse_prev = lse_in[slab, rows, :]
                top = jnp.maximum(lse_prev, mx)
                w_prev = jnp.exp2(lse_prev - top)
                w_new = jnp.exp2(mx - top)
                total = w_prev + w_new * den
                o_new = (w_prev * o_prev + w_new * acc) / total
                lse_new = top + jnp.log2(total)
            if last:
                o_out[rows, p * LANES:(p + 1) * LANES] = o_new.astype(o_out.dtype)
            else:
                dst = (res % next_residues) * LANE_GROUPS + p
                dst_rows = pl.ds(HOP * r0 + res // next_residues, BLOCK, stride=HOP)
                o_scr[dst, dst_rows, :] = o_new
                lse_out[dst, dst_rows, :] = lse_new

    def residue(res):
        block(res, 0, False)
        if n_blocks > 1:
            def body(i, carry):
                block(res, i, True)
                return carry
            lax.fori_loop(1, n_blocks, body, 0, unroll=BLOCK_UNROLL)

    if residues == 1:
        residue(0)
    else:
        def res_body(res, carry):
            residue(res)
            return carry
        lax.fori_loop(0, residues, res_body, 0, unroll=RESIDUE_UNROLL if n_blocks == 1 else MULTI_BLOCK_RESIDUE_UNROLL)

    if not last:
        for s in range(o_scr.shape[0]):
            o_out[s] = o_scr[s].astype(o_out.dtype)


def _attn_call(q, k, v, state, *, first, last):
    batch, tiles, slabs, rows_per_tile, _ = q.shape
    residues = slabs // LANE_GROUPS
    seq = tiles * rows_per_tile
    assert seq % BLOCK == 0
    qkv_spec = pl.BlockSpec((None,) + q.shape[1:], lambda b: (b, 0, 0, 0, 0))
    spec = pl.BlockSpec((None, slabs, seq, LANES), lambda b: (b, 0, 0, 0))
    args = [q, k, v]
    in_specs = [qkv_spec, qkv_spec, qkv_spec]
    if not first:
        args += list(state)
        in_specs += [spec, spec]
    scratch = []
    if last:
        width = slabs * LANES
        out_shape = [jax.ShapeDtypeStruct((batch, seq, width), BF16)]
        out_specs = [pl.BlockSpec((None, seq, width), lambda b: (b, 0, 0))]
    else:
        nxt = (slabs // HOP, seq * HOP, LANES)
        nxt_spec = pl.BlockSpec((None,) + nxt, lambda b: (b, 0, 0, 0))
        out_shape = [jax.ShapeDtypeStruct((batch,) + nxt, BF16), jax.ShapeDtypeStruct((batch,) + nxt, F32)]
        out_specs = [nxt_spec, nxt_spec]
        scratch = [pltpu.VMEM(nxt, F32)]
    return pl.pallas_call(
        functools.partial(_attn_kernel, residues=residues, n_blocks=seq // BLOCK, first=first, last=last),
        grid=(batch,),
        in_specs=in_specs,
        out_specs=out_specs,
        out_shape=out_shape,
        scratch_shapes=scratch,
        compiler_params=pltpu.CompilerParams(
            dimension_semantics=("arbitrary",), vmem_limit_bytes=VMEM_LIMIT_BYTES),
        name=f"attn_d{residues}",
    )(*args)


def _mlp_kernel(x_ref, a_ref, ug_first, ug_next, lng_ref, lnb_ref, ws_ref, bs_ref, mg_ref, ag_ref,
                wo_ref, g2_ref, w1_ref, w2_ref, out_ref, gate_scr, m_scr):
    step = pl.program_id(0)
    n_chunks = MLP_ROWS // CHUNK

    def gate_group(ug_ref, gi):
        cols = slice(gi * GMLP_GROUP_DIM, (gi + 1) * GMLP_GROUP_DIM)
        u = jax.nn.gelu(ug_ref[:, gi * GMLP_GROUP_DIM:(gi + 1) * GMLP_GROUP_DIM].astype(F32))
        gg = jax.nn.gelu(ug_ref[:, GMLP_WIDTH + gi * GMLP_GROUP_DIM:
                                GMLP_WIDTH + (gi + 1) * GMLP_GROUP_DIM].astype(F32))
        mu = jnp.mean(gg, axis=-1, keepdims=True)
        xc = gg - mu
        var = jnp.mean(xc * xc, axis=-1, keepdims=True)
        y = (xc * lax.rsqrt(var + EPS) * lng_ref[:, cols] + lnb_ref[:, cols]).astype(BF16)
        y_wide = jnp.concatenate([y[c * CHUNK:(c + 1) * CHUNK, :] for c in range(n_chunks)], axis=1)
        z_wide = jnp.dot(ws_ref[gi], y_wide, preferred_element_type=F32)
        for c in range(n_chunks):
            rows = slice(c * CHUNK, (c + 1) * CHUNK)
            z = z_wide[:, c * GMLP_GROUP_DIM:(c + 1) * GMLP_GROUP_DIM] + bs_ref[:, cols]
            gate_scr[rows, cols] = u[rows, :] * z

    def gate_norm(slot):
        mm = gate_scr[...]
        mms = jnp.mean(mm * mm, axis=-1, keepdims=True)
        m_scr[slot] = (mm * lax.rsqrt(mms + EPS) * mg_ref[...]).astype(BF16)

    @pl.when(step == 0)
    def _():
        for gi in range(GMLP_GROUPS):
            gate_group(ug_first, gi)
        gate_norm(0)

    cur = step % 2
    a = a_ref[...].astype(F32)
    ams = jnp.mean(a * a, axis=-1, keepdims=True)
    a_n = (a * lax.rsqrt(ams + EPS) * ag_ref[...]).astype(BF16)
    x1 = x_ref[...]
    x1 = x1 + jnp.dot(a_n, wo_ref[:ATTN_WIDTH, :], preferred_element_type=F32)
    x1 = x1 + jnp.dot(m_scr[cur], wo_ref[ATTN_WIDTH:, :], preferred_element_type=F32)
    ms = jnp.mean(x1 * x1, axis=-1, keepdims=True)
    h = (x1 * lax.rsqrt(ms + EPS) * g2_ref[...]).astype(BF16)
    out_ref[...] = x1
    n_ff = w1_ref.shape[1] // FF_CHUNK
    assert n_ff == GMLP_GROUPS
    for c in range(n_ff):
        cols = slice(c * FF_CHUNK, (c + 1) * FF_CHUNK)
        f = jnp.maximum(jnp.dot(h, w1_ref[:, cols], preferred_element_type=F32), 0.0)
        gate_group(ug_next, c)
        if c == n_ff - 1:
            gate_norm(1 - cur)
        out_ref[...] += jnp.dot((f * f).astype(BF16), w2_ref[cols, :], preferred_element_type=F32)


def _mlp_call(x2, a2, ug2, lng, lnb, ws, bs, mg, ag, w_out, g2, w_ff1, w_ff2):
    n_tok, d_model = x2.shape
    d_ff = w_ff1.shape[1]
    n_tiles = n_tok // MLP_ROWS
    return pl.pallas_call(
        _mlp_kernel,
        grid=(n_tiles,),
        in_specs=[
            pl.BlockSpec((MLP_ROWS, d_model), lambda i: (i, 0)),
            pl.BlockSpec((MLP_ROWS, ATTN_WIDTH), lambda i: (i, 0)),
            pl.BlockSpec((MLP_ROWS, 2 * GMLP_WIDTH), lambda i: (0, 0)),
            pl.BlockSpec((MLP_ROWS, 2 * GMLP_WIDTH), lambda i: (jnp.minimum(i + 1, n_tiles - 1), 0)),
            _const_spec((1, GMLP_WIDTH)),
            _const_spec((1, GMLP_WIDTH)),
            _const_spec((GMLP_GROUPS, CHUNK, CHUNK)),
            _const_spec((CHUNK, GMLP_WIDTH)),
            _const_spec((1, GMLP_WIDTH)),
            _const_spec((1, ATTN_WIDTH)),
            _const_spec((ATTN_WIDTH + GMLP_WIDTH, d_model)),
            _const_spec((1, d_model)),
            _const_spec((d_model, d_ff)),
            _const_spec((d_ff, d_model)),
        ],
        out_specs=pl.BlockSpec((MLP_ROWS, d_model), lambda i: (i, 0)),
        out_shape=jax.ShapeDtypeStruct((n_tok, d_model), F32),
        scratch_shapes=[pltpu.VMEM((MLP_ROWS, GMLP_WIDTH), F32),
                        pltpu.VMEM((2, MLP_ROWS, GMLP_WIDTH), BF16)],
        compiler_params=pltpu.CompilerParams(
            dimension_semantics=("arbitrary",), vmem_limit_bytes=VMEM_LIMIT_BYTES),
        name="outproj_mlp",
    )(x2, a2, ug2, ug2, lng, lnb, ws, bs, mg, ag, w_out, g2, w_ff1, w_ff2)


def kernel(x, norm1_g, w_in, q_norm_g, k_norm_g, ln_v_g, ln_v_b, w_spatial, b_spatial, attn_out_g,
           gmlp_out_g, w_out, norm2_g, w_ff1, w_ff2):
    batch, seq, d_model = x.shape
    depth = w_in.shape[0]
    assert DILATIONS == (1, HOP, HOP * HOP)
    assert all(window // dilation == BLOCK for window, dilation in WINDOW_DILATIONS)
    head_id = jnp.arange(MXU_WIDTH) // HEAD_DIM
    block_diag_mean = jnp.where(head_id[:, None] == head_id[None, :], 1.0 / HEAD_DIM, 0.0).astype(BF16)
    causal = jnp.tril(jnp.ones((CHUNK, CHUNK), dtype=bool))
    score_scale = math.log2(math.e) / math.sqrt(HEAD_DIM)
    for l in range(depth):
        qg = (jnp.tile(q_norm_g[l], ATTN_HEADS) * score_scale).reshape(1, ATTN_WIDTH)
        kg = jnp.tile(k_norm_g[l], ATTN_HEADS).reshape(1, ATTN_WIDTH)
        ws = jnp.where(causal[None], w_spatial[l], 0.0).astype(BF16)
        bs = jnp.repeat(b_spatial[l].T, GMLP_GROUP_DIM, axis=1)
        *qkv, ug3 = _proj_call(x, norm1_g[l].reshape(1, d_model), w_in[l].astype(BF16), block_diag_mean, qg, kg)
        state = None
        n_branches = len(DILATIONS)
        for bi in reversed(range(n_branches)):
            state = _attn_call(*qkv[3 * bi:3 * bi + 3], state, first=bi == n_branches - 1, last=bi == 0)
        x2 = _mlp_call(x.reshape(batch * seq, d_model), state[0].reshape(batch * seq, ATTN_WIDTH),
                       ug3.reshape(batch * seq, 2 * GMLP_WIDTH), ln_v_g[l].reshape(1, GMLP_WIDTH),
                       ln_v_b[l].reshape(1, GMLP_WIDTH), ws, bs, gmlp_out_g[l].reshape(1, GMLP_WIDTH),
                       attn_out_g[l].reshape(1, ATTN_WIDTH), w_out[l].astype(BF16),
                       norm2_g[l].reshape(1, d_model), w_ff1[l].astype(BF16), w_ff2[l].astype(BF16))
        x = x2.reshape(batch, seq, d_model)
    return x
```

```python
import functools
import math

import jax
import jax.numpy as jnp
from jax import lax
from jax.experimental import pallas as pl
from jax.experimental.pallas import tpu as pltpu

ATTN_HEADS = 8
HEAD_DIM = 64
ATTN_WIDTH = ATTN_HEADS * HEAD_DIM
GMLP_GROUPS = 4
GMLP_GROUP_DIM = 128
GMLP_WIDTH = GMLP_GROUPS * GMLP_GROUP_DIM
CHUNK = 128
BLOCK = 128
WINDOW_DILATIONS = ((128, 1), (512, 4), (2048, 16))
DILATIONS = tuple(d for _, d in WINDOW_DILATIONS)
HOP = 4
EPS = 1e-6

LANES = 128
MXU_WIDTH = 256
HEADS_PER_LANE_GROUP = LANES // HEAD_DIM
LANE_GROUPS = ATTN_WIDTH // LANES
PROJ_ROWS = 512
MLP_ROWS = 512
BLOCK_UNROLL = 15
RESIDUE_UNROLL = 4
MULTI_BLOCK_RESIDUE_UNROLL = 2
FF_CHUNK = 1024
VMEM_LIMIT_BYTES = 56 * 1024 * 1024

F32 = jnp.float32
BF16 = jnp.bfloat16


def _const_spec(shape):
    return pl.BlockSpec(shape, lambda *_: (0,) * len(shape))


def _proj_kernel(x_ref, g1_ref, w_ref, bd_ref, qg_ref, kg_ref, wo_f32, w1_f32, w2_f32,
                 qn, kn, vn, qm, km, vm, qw, kw, vw, ug_out, wo_bf16, w1_bf16, w2_bf16,
                 h_scr, raw_scr, nat_scr, mid_scr):
    step = pl.program_id(0)

    def normalise(h_out, half):
        rows = slice(half * (PROJ_ROWS // 2), (half + 1) * (PROJ_ROWS // 2))
        xf = x_ref[rows, :]
        ms = jnp.mean(xf * xf, axis=-1, keepdims=True)
        h_out[rows, :] = (xf * lax.rsqrt(ms + EPS) * g1_ref[...]).astype(BF16)

    def project(h_in, raw, c0):
        raw[:, c0:c0 + MXU_WIDTH] = jnp.dot(h_in[...], w_ref[:, c0:c0 + MXU_WIDTH],
                                            preferred_element_type=F32)

    def head_norm(t, g_ref):
        sq = (t * t).astype(BF16)
        width = bd_ref.shape[0]
        msq = jnp.concatenate(
            [jnp.dot(sq[:, c:c + width], bd_ref[...], preferred_element_type=F32)
             for c in range(0, ATTN_WIDTH, width)], axis=1)
        return t * lax.rsqrt(msq + EPS) * g_ref[...]

    def emit(t_idx, val, out_n, out_m, out_w):
        sub = PROJ_ROWS // HOP
        subsub = sub // HOP
        for p in range(LANE_GROUPS):
            slab = t_idx * LANE_GROUPS + p
            part = val[:, p * LANES:(p + 1) * LANES]
            out_n[p] = part.astype(BF16)
            nat_scr[slab] = part
        for p in range(LANE_GROUPS):
            slab = t_idx * LANE_GROUPS + p
            for r in range(HOP):
                y = nat_scr[slab, pl.ds(r, sub, stride=HOP), :]
                out_m[r * LANE_GROUPS + p] = y.astype(BF16)
                mid_scr[slab, r * sub:(r + 1) * sub, :] = y
        for p in range(LANE_GROUPS):
            slab = t_idx * LANE_GROUPS + p
            for r in range(HOP):
                for a in range(HOP):
                    z = mid_scr[slab, pl.ds(r * sub + a, subsub, stride=HOP), :]
                    out_w[(a * HOP + r) * LANE_GROUPS + p] = z.astype(BF16)

    gate_cols = 3 * ATTN_WIDTH

    def gate_raw(raw, half):
        cols = slice(half * GMLP_WIDTH, (half + 1) * GMLP_WIDTH)
        ug_out[:, cols] = raw[:, gate_cols + half * GMLP_WIDTH:gate_cols + (half + 1) * GMLP_WIDTH].astype(BF16)

    def finish_q(raw):
        emit(0, head_norm(raw[:, :ATTN_WIDTH], qg_ref), qn, qm, qw)

    def finish_k(raw):
        emit(1, head_norm(raw[:, ATTN_WIDTH:2 * ATTN_WIDTH], kg_ref), kn, km, kw)

    def finish_v(raw):
        emit(2, raw[:, 2 * ATTN_WIDTH:3 * ATTN_WIDTH], vn, vm, vw)

    def nothing(raw):
        del raw

    def cast_mlp_weights(raw):
        del raw
        wo_bf16[...] = wo_f32[...].astype(BF16)
        w1_bf16[...] = w1_f32[...].astype(BF16)
        w2_bf16[...] = w2_f32[...].astype(BF16)

    finish_parts = (finish_q, cast_mlp_weights, nothing, finish_k, nothing, nothing, finish_v,
                    functools.partial(gate_raw, half=0), functools.partial(gate_raw, half=1), nothing)

    def pipeline(cur, nxt):
        assert len(finish_parts) == w_ref.shape[1] // MXU_WIDTH
        for gi, part in enumerate(finish_parts):
            project(h_scr.at[cur], raw_scr.at[cur], gi * MXU_WIDTH)
            if gi < 2:
                normalise(h_scr.at[nxt], gi)
            part(raw_scr.at[nxt])

    @pl.when(step == 0)
    def _():
        h_scr[...] = jnp.zeros_like(h_scr)
        raw_scr[...] = jnp.zeros_like(raw_scr)

    @pl.when(step % 2 == 0)
    def _():
        pipeline(0, 1)

    @pl.when(step % 2 == 1)
    def _():
        pipeline(1, 0)


def _layout_shape(batch, seq, dilation):
    tiles = seq // PROJ_ROWS
    return (batch, tiles, dilation * LANE_GROUPS, seq // dilation // tiles, LANES)


def _proj_call(x3, g1, w_in, bd, qg, kg, mlp_weights):
    batch, seq, d_model = x3.shape
    in_width = w_in.shape[1]
    tiles = seq // PROJ_ROWS
    n_tiles = batch * tiles

    def chunk_spec(w):
        rows = w.shape[0] // n_tiles
        assert rows * n_tiles == w.shape[0] and rows % 16 == 0
        return pl.BlockSpec((rows, w.shape[1]), lambda i: (jnp.minimum(i, n_tiles - 1), 0))

    weight_specs = [chunk_spec(w) for w in mlp_weights]

    def in_tile(i):
        t = jnp.minimum(i, n_tiles - 1)
        return t // tiles, t % tiles

    def out_tile(i):
        t = jnp.maximum(i - 2, 0)
        return t // tiles, t % tiles

    def layout_index(i):
        b, j = out_tile(i)
        return b, j, 0, 0, 0

    layout_shapes, layout_specs = [], []
    for dilation in DILATIONS:
        shape = _layout_shape(batch, seq, dilation)
        layout_shapes += [jax.ShapeDtypeStruct(shape, BF16)] * 3
        layout_specs += [pl.BlockSpec((None, None) + shape[2:], layout_index)] * 3
    n_slabs = 3 * LANE_GROUPS
    return pl.pallas_call(
        _proj_kernel,
        grid=(n_tiles + 2,),
        in_specs=[
            pl.BlockSpec((None, PROJ_ROWS, d_model), lambda i: in_tile(i) + (0,)),
            _const_spec((1, d_model)),
            _const_spec((d_model, in_width)),
            _const_spec(bd.shape),
            _const_spec((1, ATTN_WIDTH)),
            _const_spec((1, ATTN_WIDTH)),
        ] + weight_specs,
        out_specs=(layout_specs
                   + [pl.BlockSpec((None, PROJ_ROWS, 2 * GMLP_WIDTH), lambda i: out_tile(i) + (0,))]
                   + weight_specs),
        out_shape=(layout_shapes + [jax.ShapeDtypeStruct((batch, seq, 2 * GMLP_WIDTH), BF16)]
                   + [jax.ShapeDtypeStruct(w.shape, BF16) for w in mlp_weights]),
        scratch_shapes=[pltpu.VMEM((2, PROJ_ROWS, d_model), BF16),
                        pltpu.VMEM((2, PROJ_ROWS, in_width), F32),
                        pltpu.VMEM((n_slabs, PROJ_ROWS, LANES), F32),
                        pltpu.VMEM((n_slabs, PROJ_ROWS, LANES), F32)],
        compiler_params=pltpu.CompilerParams(
            dimension_semantics=("arbitrary",), vmem_limit_bytes=VMEM_LIMIT_BYTES),
        name="proj_gmlp",
    )(x3, g1, w_in, bd, qg, kg, *mlp_weights)


def _attn_kernel(q_ref, k_ref, v_ref, *rest, residues, n_blocks, first, last):
    if first:
        o_in = lse_in = None
        o_out, lse_out, o_scr = rest
    elif last:
        o_in, lse_in, o_out = rest
        lse_out = o_scr = None
    else:
        o_in, lse_in, o_out, lse_out, o_scr = rest
    next_residues = max(residues // HOP, 1)

    low_half = lax.broadcasted_iota(jnp.int32, (BLOCK, LANES), 1) < HEAD_DIM

    def head_lanes(n_keys):
        low = lax.broadcasted_iota(jnp.int32, (n_keys, LANES), 1) < HEAD_DIM
        high = jnp.logical_not(low)
        return low, high, jnp.where(low, 1.0, 0.0).astype(BF16), jnp.where(high, 1.0, 0.0).astype(BF16)

    r2 = lax.broadcasted_iota(jnp.int32, (BLOCK, 2 * BLOCK), 0)
    c2 = lax.broadcasted_iota(jnp.int32, (BLOCK, 2 * BLOCK), 1)
    r1 = lax.broadcasted_iota(jnp.int32, (BLOCK, BLOCK), 0)
    c1 = lax.broadcasted_iota(jnp.int32, (BLOCK, BLOCK), 1)
    tile_consts = {
        True: (jnp.where((c2 >= r2) & (c2 <= r2 + BLOCK), 0.0, -jnp.inf).astype(F32),) + head_lanes(2 * BLOCK),
        False: (jnp.where(c1 <= r1, 0.0, -jnp.inf).astype(F32),) + head_lanes(BLOCK),
    }

    rows_per_tile = q_ref.shape[2]

    def load_block(ref, slab, blk):
        if rows_per_tile < BLOCK:
            assert n_blocks == 1 and rows_per_tile * ref.shape[0] == BLOCK
            return ref[:, slab, :, :].reshape(BLOCK, LANES)
        per_tile = rows_per_tile // BLOCK
        return ref[blk // per_tile, slab, pl.ds(pl.multiple_of((blk % per_tile) * BLOCK, BLOCK), BLOCK), :]

    def pair_tile(slab, blk, has_prev):
        bias, low, high, ones_low, ones_high = tile_consts[has_prev]
        qt = load_block(q_ref, slab, blk)
        kk = load_block(k_ref, slab, blk)
        vv = load_block(v_ref, slab, blk)
        if has_prev:
            kk = jnp.concatenate([load_block(k_ref, slab, blk - 1), kk], axis=0)
            vv = jnp.concatenate([load_block(v_ref, slab, blk - 1), vv], axis=0)
        n_keys = kk.shape[0]
        zero = jnp.zeros_like(kk)
        k2 = jnp.concatenate([jnp.where(low, kk, zero), jnp.where(high, kk, zero)], axis=0)
        v2 = jnp.concatenate([
            jnp.concatenate([jnp.where(low, vv, zero), ones_low], axis=1),
            jnp.concatenate([jnp.where(high, vv, zero), ones_high], axis=1)], axis=0)
        s = lax.dot_general(qt, k2, (((1,), (1,)), ((), ())), preferred_element_type=F32)
        s0 = s[:, :n_keys] + bias
        s1 = s[:, n_keys:] + bias
        m0 = jnp.max(s0, axis=-1, keepdims=True)
        m1 = jnp.max(s1, axis=-1, keepdims=True)
        prob = jnp.concatenate([jnp.exp2(s0 - m0), jnp.exp2(s1 - m1)], axis=1).astype(BF16)
        pv = jnp.dot(prob, v2, preferred_element_type=F32)
        return pv[:, :LANES], pv[:, LANES:], jnp.where(low_half, m0, m1)

    def block(res, blk, has_prev):
        r0 = blk * BLOCK if isinstance(blk, int) else pl.multiple_of(blk * BLOCK, BLOCK)
        rows = pl.ds(r0, BLOCK)
        for p in range(LANE_GROUPS):
            slab = res * LANE_GROUPS + p
            acc, den, mx = pair_tile(slab, blk, has_prev)
            if first:
                o_new = acc / den
                lse_new = mx + jnp.log2(den)
            else:
                o_prev = o_in[slab, rows, :].astype(F32)
                lse_prev = lse_in[slab, rows, :]
                top = jnp.maximum(lse_prev, mx)
                w_prev = jnp.exp2(lse_prev - top)
                w_new = jnp.exp2(mx - top)
                total = w_prev + w_new * den
                o_new = (w_prev * o_prev + w_new * acc) / total
                lse_new = top + jnp.log2(total)
            if last:
                o_out[rows, p * LANES:(p + 1) * LANES] = o_new.astype(o_out.dtype)
            else:
                dst = (res % next_residues) * LANE_GROUPS + p
                dst_rows = pl.ds(HOP * r0 + res // next_residues, BLOCK, stride=HOP)
                o_scr[dst, dst_rows, :] = o_new
                lse_out[dst, dst_rows, :] = lse_new

    def residue(res):
        block(res, 0, False)
        if n_blocks > 1:
            def body(i, carry):
                block(res, i, True)
                return carry
            lax.fori_loop(1, n_blocks, body, 0, unroll=BLOCK_UNROLL)

    if residues == 1:
        residue(0)
    else:
        def res_body(res, carry):
            residue(res)
            return carry
        lax.fori_loop(0, residues, res_body, 0, unroll=RESIDUE_UNROLL if n_blocks == 1 else MULTI_BLOCK_RESIDUE_UNROLL)

    if not last:
        for s in range(o_scr.shape[0]):
            o_out[s] = o_scr[s].astype(o_out.dtype)


def _attn_call(q, k, v, state, *, first, last):
    batch, tiles, slabs, rows_per_tile, _ = q.shape
    residues = slabs // LANE_GROUPS
    seq = tiles * rows_per_tile
    assert seq % BLOCK == 0
    qkv_spec = pl.BlockSpec((None,) + q.shape[1:], lambda b: (b, 0, 0, 0, 0))
    spec = pl.BlockSpec((None, slabs, seq, LANES), lambda b: (b, 0, 0, 0))
    args = [q, k, v]
    in_specs = [qkv_spec, qkv_spec, qkv_spec]
    if not first:
        args += list(state)
        in_specs += [spec, spec]
    scratch = []
    if last:
        width = slabs * LANES
        out_shape = [jax.ShapeDtypeStruct((batch, seq, width), BF16)]
        out_specs = [pl.BlockSpec((None, seq, width), lambda b: (b, 0, 0))]
    else:
        nxt = (slabs // HOP, seq * HOP, LANES)
        nxt_spec = pl.BlockSpec((None,) + nxt, lambda b: (b, 0, 0, 0))
        out_shape = [jax.ShapeDtypeStruct((batch,) + nxt, BF16), jax.ShapeDtypeStruct((batch,) + nxt, F32)]
        out_specs = [nxt_spec, nxt_spec]
        scratch = [pltpu.VMEM(nxt, F32)]
    return pl.pallas_call(
        functools.partial(_attn_kernel, residues=residues, n_blocks=seq // BLOCK, first=first, last=last),
        grid=(batch,),
        in_specs=in_specs,
        out_specs=out_specs,
        out_shape=out_shape,
        scratch_shapes=scratch,
        compiler_params=pltpu.CompilerParams(
            dimension_semantics=("arbitrary",), vmem_limit_bytes=VMEM_LIMIT_BYTES),
        name=f"attn_d{residues}",
    )(*args)


def _mlp_kernel(x_ref, a_ref, ug_first, ug_next, lng_ref, lnb_ref, ws_ref, bs_ref, mg_ref, ag_ref,
                wo_ref, g2_ref, w1_ref, w2_ref, out_ref, gate_scr, m_scr):
    step = pl.program_id(0)
    n_chunks = MLP_ROWS // CHUNK

    def gate_group(ug_ref, gi):
        cols = slice(gi * GMLP_GROUP_DIM, (gi + 1) * GMLP_GROUP_DIM)
        u = jax.nn.gelu(ug_ref[:, gi * GMLP_GROUP_DIM:(gi + 1) * GMLP_GROUP_DIM].astype(F32))
        gg = jax.nn.gelu(ug_ref[:, GMLP_WIDTH + gi * GMLP_GROUP_DIM:
                                GMLP_WIDTH + (gi + 1) * GMLP_GROUP_DIM].astype(F32))
        mu = jnp.mean(gg, axis=-1, keepdims=True)
        xc = gg - mu
        var = jnp.mean(xc * xc, axis=-1, keepdims=True)
        y = (xc * lax.rsqrt(var + EPS) * lng_ref[:, cols] + lnb_ref[:, cols]).astype(BF16)
        y_wide = jnp.concatenate([y[c * CHUNK:(c + 1) * CHUNK, :] for c in range(n_chunks)], axis=1)
        z_wide = jnp.dot(ws_ref[gi], y_wide, preferred_element_type=F32)
        for c in range(n_chunks):
            rows = slice(c * CHUNK, (c + 1) * CHUNK)
            z = z_wide[:, c * GMLP_GROUP_DIM:(c + 1) * GMLP_GROUP_DIM] + bs_ref[:, cols]
            gate_scr[rows, cols] = u[rows, :] * z

    def gate_norm(slot):
        mm = gate_scr[...]
        mms = jnp.mean(mm * mm, axis=-1, keepdims=True)
        m_scr[slot] = (mm * lax.rsqrt(mms + EPS) * mg_ref[...]).astype(BF16)

    @pl.when(step == 0)
    def _():
        for gi in range(GMLP_GROUPS):
            gate_group(ug_first, gi)
        gate_norm(0)

    cur = step % 2
    a = a_ref[...].astype(F32)
    ams = jnp.mean(a * a, axis=-1, keepdims=True)
    a_n = (a * lax.rsqrt(ams + EPS) * ag_ref[...]).astype(BF16)
    x1 = x_ref[...]
    x1 = x1 + jnp.dot(a_n, wo_ref[:ATTN_WIDTH, :], preferred_element_type=F32)
    x1 = x1 + jnp.dot(m_scr[cur], wo_ref[ATTN_WIDTH:, :], preferred_element_type=F32)
    ms = jnp.mean(x1 * x1, axis=-1, keepdims=True)
    h = (x1 * lax.rsqrt(ms + EPS) * g2_ref[...]).astype(BF16)
    out_ref[...] = x1
    n_ff = w1_ref.shape[1] // FF_CHUNK
    assert n_ff == GMLP_GROUPS
    for c in range(n_ff):
        cols = slice(c * FF_CHUNK, (c + 1) * FF_CHUNK)
        f = jnp.maximum(jnp.dot(h, w1_ref[:, cols], preferred_element_type=F32), 0.0)
        gate_group(ug_next, c)
        if c == n_ff - 1:
            gate_norm(1 - cur)
        out_ref[...] += jnp.dot((f * f).astype(BF16), w2_ref[cols, :], preferred_element_type=F32)


def _mlp_call(x2, a2, ug2, lng, lnb, ws, bs, mg, ag, w_out, g2, w_ff1, w_ff2):
    n_tok, d_model = x2.shape
    d_ff = w_ff1.shape[1]
    n_tiles = n_tok // MLP_ROWS
    return pl.pallas_call(
        _mlp_kernel,
        grid=(n_tiles,),
        in_specs=[
            pl.BlockSpec((MLP_ROWS, d_model), lambda i: (i, 0)),
            pl.BlockSpec((MLP_ROWS, ATTN_WIDTH), lambda i: (i, 0)),
            pl.BlockSpec((MLP_ROWS, 2 * GMLP_WIDTH), lambda i: (0, 0)),
            pl.BlockSpec((MLP_ROWS, 2 * GMLP_WIDTH), lambda i: (jnp.minimum(i + 1, n_tiles - 1), 0)),
            _const_spec((1, GMLP_WIDTH)),
            _const_spec((1, GMLP_WIDTH)),
            _const_spec((GMLP_GROUPS, CHUNK, CHUNK)),
            _const_spec((CHUNK, GMLP_WIDTH)),
            _const_spec((1, GMLP_WIDTH)),
            _const_spec((1, ATTN_WIDTH)),
            _const_spec((ATTN_WIDTH + GMLP_WIDTH, d_model)),
            _const_spec((1, d_model)),
            _const_spec((d_model, d_ff)),
            _const_spec((d_ff, d_model)),
        ],
        out_specs=pl.BlockSpec((MLP_ROWS, d_model), lambda i: (i, 0)),
        out_shape=jax.ShapeDtypeStruct((n_tok, d_model), F32),
        scratch_shapes=[pltpu.VMEM((MLP_ROWS, GMLP_WIDTH), F32),
                        pltpu.VMEM((2, MLP_ROWS, GMLP_WIDTH), BF16)],
        compiler_params=pltpu.CompilerParams(
            dimension_semantics=("arbitrary",), vmem_limit_bytes=VMEM_LIMIT_BYTES),
        name="outproj_mlp",
    )(x2, a2, ug2, ug2, lng, lnb, ws, bs, mg, ag, w_out, g2, w_ff1, w_ff2)


def kernel(x, norm1_g, w_in, q_norm_g, k_norm_g, ln_v_g, ln_v_b, w_spatial, b_spatial, attn_out_g,
           gmlp_out_g, w_out, norm2_g, w_ff1, w_ff2):
    batch, seq, d_model = x.shape
    depth = w_in.shape[0]
    assert DILATIONS == (1, HOP, HOP * HOP)
    assert all(window // dilation == BLOCK for window, dilation in WINDOW_DILATIONS)
    head_id = jnp.arange(MXU_WIDTH) // HEAD_DIM
    block_diag_mean = jnp.where(head_id[:, None] == head_id[None, :], 1.0 / HEAD_DIM, 0.0).astype(BF16)
    causal = jnp.tril(jnp.ones((CHUNK, CHUNK), dtype=bool))
    score_scale = math.log2(math.e) / math.sqrt(HEAD_DIM)
    for l in range(depth):
        qg = (jnp.tile(q_norm_g[l], ATTN_HEADS) * score_scale).reshape(1, ATTN_WIDTH)
        kg = jnp.tile(k_norm_g[l], ATTN_HEADS).reshape(1, ATTN_WIDTH)
        ws = jnp.where(causal[None], w_spatial[l], 0.0).astype(BF16)
        bs = jnp.repeat(b_spatial[l].T, GMLP_GROUP_DIM, axis=1)
        *qkv, ug3, wo_b, w1_b, w2_b = _proj_call(
            x, norm1_g[l].reshape(1, d_model), w_in[l].astype(BF16), block_diag_mean, qg, kg,
            (w_out[l], w_ff1[l], w_ff2[l]))
        state = None
        n_branches = len(DILATIONS)
        for bi in reversed(range(n_branches)):
            state = _attn_call(*qkv[3 * bi:3 * bi + 3], state, first=bi == n_branches - 1, last=bi == 0)
        x2 = _mlp_call(x.reshape(batch * seq, d_model), state[0].reshape(batch * seq, ATTN_WIDTH),
                       ug3.reshape(batch * seq, 2 * GMLP_WIDTH), ln_v_g[l].reshape(1, GMLP_WIDTH),
                       ln_v_b[l].reshape(1, GMLP_WIDTH), ws, bs, gmlp_out_g[l].reshape(1, GMLP_WIDTH),
                       attn_out_g[l].reshape(1, ATTN_WIDTH), wo_b, norm2_g[l].reshape(1, d_model), w1_b, w2_b)
        x = x2.reshape(batch, seq, d_model)
    return x
```

```python
import functools
import math

import jax
import jax.numpy as jnp
from jax import lax
from jax.experimental import pallas as pl
from jax.experimental.pallas import tpu as pltpu

ATTN_HEADS = 8
HEAD_DIM = 64
ATTN_WIDTH = ATTN_HEADS * HEAD_DIM
GMLP_GROUPS = 4
GMLP_GROUP_DIM = 128
GMLP_WIDTH = GMLP_GROUPS * GMLP_GROUP_DIM
CHUNK = 128
BLOCK = 128
WINDOW_DILATIONS = ((128, 1), (512, 4), (2048, 16))
DILATIONS = tuple(d for _, d in WINDOW_DILATIONS)
HOP = 4
EPS = 1e-6

LANES = 128
MXU_WIDTH = 256
HEADS_PER_LANE_GROUP = LANES // HEAD_DIM
LANE_GROUPS = ATTN_WIDTH // LANES
PROJ_ROWS = 512
MLP_ROWS = 1024
BLOCK_UNROLL = 15
RESIDUE_UNROLL = 4
MULTI_BLOCK_RESIDUE_UNROLL = 2
FF_CHUNK = 1024
VMEM_LIMIT_BYTES = 56 * 1024 * 1024

F32 = jnp.float32
BF16 = jnp.bfloat16


def _const_spec(shape):
    return pl.BlockSpec(shape, lambda *_: (0,) * len(shape))


def _proj_kernel(x_ref, g1_ref, w_ref, bd_ref, qg_ref, kg_ref, wo_f32, w1_f32, w2_f32,
                 qn, kn, vn, qm, km, vm, qw, kw, vw, ug_out, wo_bf16, w1_bf16, w2_bf16,
                 h_scr, raw_scr, nat_scr, mid_scr):
    step = pl.program_id(0)

    def normalise(h_out, half):
        rows = slice(half * (PROJ_ROWS // 2), (half + 1) * (PROJ_ROWS // 2))
        xf = x_ref[rows, :]
        ms = jnp.mean(xf * xf, axis=-1, keepdims=True)
        h_out[rows, :] = (xf * lax.rsqrt(ms + EPS) * g1_ref[...]).astype(BF16)

    def project(h_in, raw, c0):
        raw[:, c0:c0 + MXU_WIDTH] = jnp.dot(h_in[...], w_ref[:, c0:c0 + MXU_WIDTH],
                                            preferred_element_type=F32)

    def head_norm(t, g_ref):
        sq = (t * t).astype(BF16)
        width = bd_ref.shape[0]
        msq = jnp.concatenate(
            [jnp.dot(sq[:, c:c + width], bd_ref[...], preferred_element_type=F32)
             for c in range(0, ATTN_WIDTH, width)], axis=1)
        return t * lax.rsqrt(msq + EPS) * g_ref[...]

    def emit(t_idx, val, out_n, out_m, out_w):
        sub = PROJ_ROWS // HOP
        subsub = sub // HOP
        for p in range(LANE_GROUPS):
            slab = t_idx * LANE_GROUPS + p
            part = val[:, p * LANES:(p + 1) * LANES]
            out_n[p] = part.astype(BF16)
            nat_scr[slab] = part
        for p in range(LANE_GROUPS):
            slab = t_idx * LANE_GROUPS + p
            for r in range(HOP):
                y = nat_scr[slab, pl.ds(r, sub, stride=HOP), :]
                out_m[r * LANE_GROUPS + p] = y.astype(BF16)
                mid_scr[slab, r * sub:(r + 1) * sub, :] = y
        for p in range(LANE_GROUPS):
            slab = t_idx * LANE_GROUPS + p
            for r in range(HOP):
                for a in range(HOP):
                    z = mid_scr[slab, pl.ds(r * sub + a, subsub, stride=HOP), :]
                    out_w[(a * HOP + r) * LANE_GROUPS + p] = z.astype(BF16)

    gate_cols = 3 * ATTN_WIDTH

    def gate_raw(raw, half):
        cols = slice(half * GMLP_WIDTH, (half + 1) * GMLP_WIDTH)
        ug_out[:, cols] = raw[:, gate_cols + half * GMLP_WIDTH:gate_cols + (half + 1) * GMLP_WIDTH].astype(BF16)

    def finish_q(raw):
        emit(0, head_norm(raw[:, :ATTN_WIDTH], qg_ref), qn, qm, qw)

    def finish_k(raw):
        emit(1, head_norm(raw[:, ATTN_WIDTH:2 * ATTN_WIDTH], kg_ref), kn, km, kw)

    def finish_v(raw):
        emit(2, raw[:, 2 * ATTN_WIDTH:3 * ATTN_WIDTH], vn, vm, vw)

    def nothing(raw):
        del raw

    def cast_mlp_weights(raw):
        del raw
        wo_bf16[...] = wo_f32[...].astype(BF16)
        w1_bf16[...] = w1_f32[...].astype(BF16)
        w2_bf16[...] = w2_f32[...].astype(BF16)

    finish_parts = (finish_q, cast_mlp_weights, nothing, finish_k, nothing, nothing, finish_v,
                    functools.partial(gate_raw, half=0), functools.partial(gate_raw, half=1), nothing)

    def pipeline(cur, nxt):
        assert len(finish_parts) == w_ref.shape[1] // MXU_WIDTH
        for gi, part in enumerate(finish_parts):
            project(h_scr.at[cur], raw_scr.at[cur], gi * MXU_WIDTH)
            if gi < 2:
                normalise(h_scr.at[nxt], gi)
            part(raw_scr.at[nxt])

    @pl.when(step == 0)
    def _():
        h_scr[...] = jnp.zeros_like(h_scr)
        raw_scr[...] = jnp.zeros_like(raw_scr)

    @pl.when(step % 2 == 0)
    def _():
        pipeline(0, 1)

    @pl.when(step % 2 == 1)
    def _():
        pipeline(1, 0)


def _layout_shape(batch, seq, dilation):
    tiles = seq // PROJ_ROWS
    return (batch, tiles, dilation * LANE_GROUPS, seq // dilation // tiles, LANES)


def _proj_call(x3, g1, w_in, bd, qg, kg, mlp_weights):
    batch, seq, d_model = x3.shape
    in_width = w_in.shape[1]
    tiles = seq // PROJ_ROWS
    n_tiles = batch * tiles

    def chunk_spec(w):
        rows = w.shape[0] // n_tiles
        assert rows * n_tiles == w.shape[0] and rows % 16 == 0
        return pl.BlockSpec((rows, w.shape[1]), lambda i: (jnp.minimum(i, n_tiles - 1), 0))

    weight_specs = [chunk_spec(w) for w in mlp_weights]

    def in_tile(i):
        t = jnp.minimum(i, n_tiles - 1)
        return t // tiles, t % tiles

    def out_tile(i):
        t = jnp.maximum(i - 2, 0)
        return t // tiles, t % tiles

    def layout_index(i):
        b, j = out_tile(i)
        return b, j, 0, 0, 0

    layout_shapes, layout_specs = [], []
    for dilation in DILATIONS:
        shape = _layout_shape(batch, seq, dilation)
        layout_shapes += [jax.ShapeDtypeStruct(shape, BF16)] * 3
        layout_specs += [pl.BlockSpec((None, None) + shape[2:], layout_index)] * 3
    n_slabs = 3 * LANE_GROUPS
    return pl.pallas_call(
        _proj_kernel,
        grid=(n_tiles + 2,),
        in_specs=[
            pl.BlockSpec((None, PROJ_ROWS, d_model), lambda i: in_tile(i) + (0,)),
            _const_spec((1, d_model)),
            _const_spec((d_model, in_width)),
            _const_spec(bd.shape),
            _const_spec((1, ATTN_WIDTH)),
            _const_spec((1, ATTN_WIDTH)),
        ] + weight_specs,
        out_specs=(layout_specs
                   + [pl.BlockSpec((None, PROJ_ROWS, 2 * GMLP_WIDTH), lambda i: out_tile(i) + (0,))]
                   + weight_specs),
        out_shape=(layout_shapes + [jax.ShapeDtypeStruct((batch, seq, 2 * GMLP_WIDTH), BF16)]
                   + [jax.ShapeDtypeStruct(w.shape, BF16) for w in mlp_weights]),
        scratch_shapes=[pltpu.VMEM((2, PROJ_ROWS, d_model), BF16),
                        pltpu.VMEM((2, PROJ_ROWS, in_width), F32),
                        pltpu.VMEM((n_slabs, PROJ_ROWS, LANES), F32),
                        pltpu.VMEM((n_slabs, PROJ_ROWS, LANES), F32)],
        compiler_params=pltpu.CompilerParams(
            dimension_semantics=("arbitrary",), vmem_limit_bytes=VMEM_LIMIT_BYTES),
        name="proj_gmlp",
    )(x3, g1, w_in, bd, qg, kg, *mlp_weights)


def _attn_kernel(q_ref, k_ref, v_ref, *rest, residues, n_blocks, first, last):
    if first:
        o_in = lse_in = None
        o_out, lse_out, o_scr = rest
    elif last:
        o_in, lse_in, o_out = rest
        lse_out = o_scr = None
    else:
        o_in, lse_in, o_out, lse_out, o_scr = rest
    next_residues = max(residues // HOP, 1)

    low_half = lax.broadcasted_iota(jnp.int32, (BLOCK, LANES), 1) < HEAD_DIM

    def head_lanes(n_keys):
        low = lax.broadcasted_iota(jnp.int32, (n_keys, LANES), 1) < HEAD_DIM
        high = jnp.logical_not(low)
        return low, high, jnp.where(low, 1.0, 0.0).astype(BF16), jnp.where(high, 1.0, 0.0).astype(BF16)

    r2 = lax.broadcasted_iota(jnp.int32, (BLOCK, 2 * BLOCK), 0)
    c2 = lax.broadcasted_iota(jnp.int32, (BLOCK, 2 * BLOCK), 1)
    r1 = lax.broadcasted_iota(jnp.int32, (BLOCK, BLOCK), 0)
    c1 = lax.broadcasted_iota(jnp.int32, (BLOCK, BLOCK), 1)
    tile_consts = {
        True: (jnp.where((c2 >= r2) & (c2 <= r2 + BLOCK), 0.0, -jnp.inf).astype(F32),) + head_lanes(2 * BLOCK),
        False: (jnp.where(c1 <= r1, 0.0, -jnp.inf).astype(F32),) + head_lanes(BLOCK),
    }

    rows_per_tile = q_ref.shape[2]

    def load_block(ref, slab, blk):
        if rows_per_tile < BLOCK:
            assert n_blocks == 1 and rows_per_tile * ref.shape[0] == BLOCK
            return ref[:, slab, :, :].reshape(BLOCK, LANES)
        per_tile = rows_per_tile // BLOCK
        return ref[blk // per_tile, slab, pl.ds(pl.multiple_of((blk % per_tile) * BLOCK, BLOCK), BLOCK), :]

    def pair_tile(slab, blk, has_prev):
        bias, low, high, ones_low, ones_high = tile_consts[has_prev]
        qt = load_block(q_ref, slab, blk)
        kk = load_block(k_ref, slab, blk)
        vv = load_block(v_ref, slab, blk)
        if has_prev:
            kk = jnp.concatenate([load_block(k_ref, slab, blk - 1), kk], axis=0)
            vv = jnp.concatenate([load_block(v_ref, slab, blk - 1), vv], axis=0)
        n_keys = kk.shape[0]
        zero = jnp.zeros_like(kk)
        k2 = jnp.concatenate([jnp.where(low, kk, zero), jnp.where(high, kk, zero)], axis=0)
        v2 = jnp.concatenate([
            jnp.concatenate([jnp.where(low, vv, zero), ones_low], axis=1),
            jnp.concatenate([jnp.where(high, vv, zero), ones_high], axis=1)], axis=0)
        s = lax.dot_general(qt, k2, (((1,), (1,)), ((), ())), preferred_element_type=F32)
        s0 = s[:, :n_keys] + bias
        s1 = s[:, n_keys:] + bias
        m0 = jnp.max(s0, axis=-1, keepdims=True)
        m1 = jnp.max(s1, axis=-1, keepdims=True)
        prob = jnp.concatenate([jnp.exp2(s0 - m0), jnp.exp2(s1 - m1)], axis=1).astype(BF16)
        pv = jnp.dot(prob, v2, preferred_element_type=F32)
        return pv[:, :LANES], pv[:, LANES:], jnp.where(low_half, m0, m1)

    def block(res, blk, has_prev):
        r0 = blk * BLOCK if isinstance(blk, int) else pl.multiple_of(blk * BLOCK, BLOCK)
        rows = pl.ds(r0, BLOCK)
        for p in range(LANE_GROUPS):
            slab = res * LANE_GROUPS + p
            acc, den, mx = pair_tile(slab, blk, has_prev)
            if first:
                o_new = acc / den
                lse_new = mx + jnp.log2(den)
            else:
                o_prev = o_in[slab, rows, :].astype(F32)
                lse_prev = lse_in[slab, rows, :]
                top = jnp.maximum(lse_prev, mx)
                w_prev = jnp.exp2(lse_prev - top)
                w_new = jnp.exp2(mx - top)
                total = w_prev + w_new * den
                o_new = (w_prev * o_prev + w_new * acc) / total
                lse_new = top + jnp.log2(total)
            if last:
                o_out[rows, p * LANES:(p + 1) * LANES] = o_new.astype(o_out.dtype)
            else:
                dst = (res % next_residues) * LANE_GROUPS + p
                dst_rows = pl.ds(HOP * r0 + res // next_residues, BLOCK, stride=HOP)
                o_scr[dst, dst_rows, :] = o_new
                lse_out[dst, dst_rows, :] = lse_new

    def residue(res):
        block(res, 0, False)
        if n_blocks > 1:
            def body(i, carry):
                block(res, i, True)
                return carry
            lax.fori_loop(1, n_blocks, body, 0, unroll=BLOCK_UNROLL)

    if residues == 1:
        residue(0)
    else:
        def res_body(res, carry):
            residue(res)
            return carry
        lax.fori_loop(0, residues, res_body, 0, unroll=RESIDUE_UNROLL if n_blocks == 1 else MULTI_BLOCK_RESIDUE_UNROLL)

    if not last:
        for s in range(o_scr.shape[0]):
            o_out[s] = o_scr[s].astype(o_out.dtype)


def _attn_call(q, k, v, state, *, first, last):
    batch, tiles, slabs, rows_per_tile, _ = q.shape
    residues = slabs // LANE_GROUPS
    seq = tiles * rows_per_tile
    assert seq % BLOCK == 0
    qkv_spec = pl.BlockSpec((None,) + q.shape[1:], lambda b: (b, 0, 0, 0, 0))
    spec = pl.BlockSpec((None, slabs, seq, LANES), lambda b: (b, 0, 0, 0))
    args = [q, k, v]
    in_specs = [qkv_spec, qkv_spec, qkv_spec]
    if not first:
        args += list(state)
        in_specs += [spec, spec]
    scratch = []
    if last:
        width = slabs * LANES
        out_shape = [jax.ShapeDtypeStruct((batch, seq, width), BF16)]
        out_specs = [pl.BlockSpec((None, seq, width), lambda b: (b, 0, 0))]
    else:
        nxt = (slabs // HOP, seq * HOP, LANES)
        nxt_spec = pl.BlockSpec((None,) + nxt, lambda b: (b, 0, 0, 0))
        out_shape = [jax.ShapeDtypeStruct((batch,) + nxt, BF16), jax.ShapeDtypeStruct((batch,) + nxt, F32)]
        out_specs = [nxt_spec, nxt_spec]
        scratch = [pltpu.VMEM(nxt, F32)]
    return pl.pallas_call(
        functools.partial(_attn_kernel, residues=residues, n_blocks=seq // BLOCK, first=first, last=last),
        grid=(batch,),
        in_specs=in_specs,
        out_specs=out_specs,
        out_shape=out_shape,
        scratch_shapes=scratch,
        compiler_params=pltpu.CompilerParams(
            dimension_semantics=("arbitrary",), vmem_limit_bytes=VMEM_LIMIT_BYTES),
        name=f"attn_d{residues}",
    )(*args)


def _mlp_kernel(x_ref, a_ref, ug_first, ug_next, lng_ref, lnb_ref, ws_ref, bs_ref, mg_ref, ag_ref,
                wo_ref, g2_ref, w1_ref, w2_ref, out_ref, gate_scr, m_scr):
    step = pl.program_id(0)
    n_chunks = MLP_ROWS // CHUNK

    def gate_group(ug_ref, gi):
        cols = slice(gi * GMLP_GROUP_DIM, (gi + 1) * GMLP_GROUP_DIM)
        u = jax.nn.gelu(ug_ref[:, gi * GMLP_GROUP_DIM:(gi + 1) * GMLP_GROUP_DIM].astype(F32))
        gg = jax.nn.gelu(ug_ref[:, GMLP_WIDTH + gi * GMLP_GROUP_DIM:
                                GMLP_WIDTH + (gi + 1) * GMLP_GROUP_DIM].astype(F32))
        mu = jnp.mean(gg, axis=-1, keepdims=True)
        xc = gg - mu
        var = jnp.mean(xc * xc, axis=-1, keepdims=True)
        y = (xc * lax.rsqrt(var + EPS) * lng_ref[:, cols] + lnb_ref[:, cols]).astype(BF16)
        y_wide = jnp.concatenate([y[c * CHUNK:(c + 1) * CHUNK, :] for c in range(n_chunks)], axis=1)
        z_wide = jnp.dot(ws_ref[gi], y_wide, preferred_element_type=F32)
        for c in range(n_chunks):
            rows = slice(c * CHUNK, (c + 1) * CHUNK)
            z = z_wide[:, c * GMLP_GROUP_DIM:(c + 1) * GMLP_GROUP_DIM] + bs_ref[:, cols]
            gate_scr[rows, cols] = u[rows, :] * z

    def gate_norm(slot):
        mm = gate_scr[...]
        mms = jnp.mean(mm * mm, axis=-1, keepdims=True)
        m_scr[slot] = (mm * lax.rsqrt(mms + EPS) * mg_ref[...]).astype(BF16)

    @pl.when(step == 0)
    def _():
        for gi in range(GMLP_GROUPS):
            gate_group(ug_first, gi)
        gate_norm(0)

    cur = step % 2
    a = a_ref[...].astype(F32)
    ams = jnp.mean(a * a, axis=-1, keepdims=True)
    a_n = (a * lax.rsqrt(ams + EPS) * ag_ref[...]).astype(BF16)
    x1 = x_ref[...]
    x1 = x1 + jnp.dot(a_n, wo_ref[:ATTN_WIDTH, :], preferred_element_type=F32)
    x1 = x1 + jnp.dot(m_scr[cur], wo_ref[ATTN_WIDTH:, :], preferred_element_type=F32)
    ms = jnp.mean(x1 * x1, axis=-1, keepdims=True)
    h = (x1 * lax.rsqrt(ms + EPS) * g2_ref[...]).astype(BF16)
    out_ref[...] = x1
    n_ff = w1_ref.shape[1] // FF_CHUNK
    assert n_ff == GMLP_GROUPS
    for c in range(n_ff):
        cols = slice(c * FF_CHUNK, (c + 1) * FF_CHUNK)
        f = jnp.maximum(jnp.dot(h, w1_ref[:, cols], preferred_element_type=F32), 0.0)
        gate_group(ug_next, c)
        if c == n_ff - 1:
            gate_norm(1 - cur)
        out_ref[...] += jnp.dot((f * f).astype(BF16), w2_ref[cols, :], preferred_element_type=F32)


def _mlp_call(x2, a2, ug2, lng, lnb, ws, bs, mg, ag, w_out, g2, w_ff1, w_ff2):
    n_tok, d_model = x2.shape
    d_ff = w_ff1.shape[1]
    n_tiles = n_tok // MLP_ROWS
    return pl.pallas_call(
        _mlp_kernel,
        grid=(n_tiles,),
        in_specs=[
            pl.BlockSpec((MLP_ROWS, d_model), lambda i: (i, 0)),
            pl.BlockSpec((MLP_ROWS, ATTN_WIDTH), lambda i: (i, 0)),
            pl.BlockSpec((MLP_ROWS, 2 * GMLP_WIDTH), lambda i: (0, 0)),
            pl.BlockSpec((MLP_ROWS, 2 * GMLP_WIDTH), lambda i: (jnp.minimum(i + 1, n_tiles - 1), 0)),
            _const_spec((1, GMLP_WIDTH)),
            _const_spec((1, GMLP_WIDTH)),
            _const_spec((GMLP_GROUPS, CHUNK, CHUNK)),
            _const_spec((CHUNK, GMLP_WIDTH)),
            _const_spec((1, GMLP_WIDTH)),
            _const_spec((1, ATTN_WIDTH)),
            _const_spec((ATTN_WIDTH + GMLP_WIDTH, d_model)),
            _const_spec((1, d_model)),
            _const_spec((d_model, d_ff)),
            _const_spec((d_ff, d_model)),
        ],
        out_specs=pl.BlockSpec((MLP_ROWS, d_model), lambda i: (i, 0)),
        out_shape=jax.ShapeDtypeStruct((n_tok, d_model), F32),
        scratch_shapes=[pltpu.VMEM((MLP_ROWS, GMLP_WIDTH), F32),
                        pltpu.VMEM((2, MLP_ROWS, GMLP_WIDTH), BF16)],
        compiler_params=pltpu.CompilerParams(
            dimension_semantics=("arbitrary",), vmem_limit_bytes=VMEM_LIMIT_BYTES),
        name="outproj_mlp",
    )(x2, a2, ug2, ug2, lng, lnb, ws, bs, mg, ag, w_out, g2, w_ff1, w_ff2)


def kernel(x, norm1_g, w_in, q_norm_g, k_norm_g, ln_v_g, ln_v_b, w_spatial, b_spatial, attn_out_g,
           gmlp_out_g, w_out, norm2_g, w_ff1, w_ff2):
    batch, seq, d_model = x.shape
    depth = w_in.shape[0]
    assert DILATIONS == (1, HOP, HOP * HOP)
    assert all(window // dilation == BLOCK for window, dilation in WINDOW_DILATIONS)
    head_id = jnp.arange(MXU_WIDTH) // HEAD_DIM
    block_diag_mean = jnp.where(head_id[:, None] == head_id[None, :], 1.0 / HEAD_DIM, 0.0).astype(BF16)
    causal = jnp.tril(jnp.ones((CHUNK, CHUNK), dtype=bool))
    score_scale = math.log2(math.e) / math.sqrt(HEAD_DIM)
    for l in range(depth):
        qg = (jnp.tile(q_norm_g[l], ATTN_HEADS) * score_scale).reshape(1, ATTN_WIDTH)
        kg = jnp.tile(k_norm_g[l], ATTN_HEADS).reshape(1, ATTN_WIDTH)
        ws = jnp.where(causal[None], w_spatial[l], 0.0).astype(BF16)
        bs = jnp.repeat(b_spatial[l].T, GMLP_GROUP_DIM, axis=1)
        *qkv, ug3, wo_b, w1_b, w2_b = _proj_call(
            x, norm1_g[l].reshape(1, d_model), w_in[l].astype(BF16), block_diag_mean, qg, kg,
            (w_out[l], w_ff1[l], w_ff2[l]))
        state = None
        n_branches = len(DILATIONS)
        for bi in reversed(range(n_branches)):
            state = _attn_call(*qkv[3 * bi:3 * bi + 3], state, first=bi == n_branches - 1, last=bi == 0)
        x2 = _mlp_call(x.reshape(batch * seq, d_model), state[0].reshape(batch * seq, ATTN_WIDTH),
                       ug3.reshape(batch * seq, 2 * GMLP_WIDTH), ln_v_g[l].reshape(1, GMLP_WIDTH),
                       ln_v_b[l].reshape(1, GMLP_WIDTH), ws, bs, gmlp_out_g[l].reshape(1, GMLP_WIDTH),
                       attn_out_g[l].reshape(1, ATTN_WIDTH), wo_b, norm2_g[l].reshape(1, d_model), w1_b, w2_b)
        x = x2.reshape(batch, seq, d_model)
    return x
```
